```python
import functools
import jax
import jax.numpy as jnp
from jax import lax
import numpy as np

D_MODEL = 1024
BATCH = 4
SEQ = 8192
DEPTH = 2
DEC_BATCH = 128
DEC_SEQ = 1
PAST_LEN = 16384
PAGE_SIZE = 128

HEAD_DIM = 64
H_A = 8
Q_LORA = 384
KV_LORA = 256
NOPE = 64
ROPE_D = 32
V_DIM = 64
H_B = 8
KV_B = 4
G_B = H_B // KV_B
H_C = 8
C_WINDOWS = (128, 512, 2048)
C_DILATIONS = (1, 4, 16)
N_C_GROUPS = len(C_WINDOWS)
N_EXPERTS = 32
TOP_K = 4
D_FF = 1024
SWIGLU_ALPHA = 1.702
SWIGLU_LIMIT = 7.0
MOE_BLOCK = 128

ROPE_THETA = 10000.0
NORM_EPS = 1e-6
Q_BLOCK = 128
NEG_INF = -1e30
MLA_SCALE = (NOPE + ROPE_D) ** -0.5
HD_SCALE = HEAD_DIM ** -0.5
N_A_LAYERS = (DEPTH + 1) // 2
N_C_LAYERS = DEPTH // 2
IN_A = Q_LORA + KV_LORA + ROPE_D
IN_AB = IN_A + H_B * HEAD_DIM + 2 * KV_B * HEAD_DIM + H_B
AB_SPLITS = (Q_LORA, Q_LORA + KV_LORA, IN_A, IN_A + H_B * HEAD_DIM, IN_A + (H_B + KV_B) * HEAD_DIM, IN_A + (H_B + 2 * KV_B) * HEAD_DIM)
OUT_AB = H_A * V_DIM + H_B * HEAD_DIM
IN_C = N_C_GROUPS * 3 * H_C * HEAD_DIM
OUT_C = H_C * HEAD_DIM

kernel_name = 'hybrid_mla_fox_dilated_moe_step'


def rms_norm(x, g):
    xf = x.astype(jnp.float32)
    y = xf * lax.rsqrt(jnp.mean(xf * xf, axis=-1, keepdims=True) + NORM_EPS)
    return (y * g.astype(jnp.float32)).astype(x.dtype)


def rope(x, pos):
    half = x.shape[-1] // 2
    inv_freq = ROPE_THETA ** (-jnp.arange(half, dtype=jnp.float32) / half)
    ang = pos.astype(jnp.float32)[:, None] * inv_freq[None, :]
    ang = ang.reshape(ang.shape[0], *([1] * (x.ndim - 3)), half)
    cos, sin = jnp.cos(ang), jnp.sin(ang)
    xf = x.astype(jnp.float32)
    x1, x2 = xf[..., :half], xf[..., half:]
    return jnp.concatenate([x1 * cos - x2 * sin, x2 * cos + x1 * sin], axis=-1).astype(x.dtype)


def masked_softmax(s, mask):
    return jax.nn.softmax(jnp.where(mask, s, NEG_INF), axis=-1)


def adaln(c, w, b):
    m = jax.nn.silu(c) @ w + b
    return [m[:, None, i * D_MODEL:(i + 1) * D_MODEL] for i in range(6)]


def modulate(x, g, shift, scale):
    return rms_norm(x, g) * (1 + scale) + shift


def sweep_query_blocks(fn, q_args, seq):
    nb = seq // Q_BLOCK
    blocked = tuple(jnp.moveaxis(a.reshape(a.shape[0], nb, Q_BLOCK, *a.shape[2:]), 1, 0) for a in q_args)
    out = lax.map(lambda t: fn(*t), (jnp.arange(nb),) + blocked)
    out = jnp.moveaxis(out, 0, 1)
    return out.reshape(out.shape[0], seq, *out.shape[3:])


def gather_pages(cache, page_table):
    g = cache[page_table]
    return g.reshape(g.shape[0], -1, *g.shape[3:])


def ab_project(h, pos, w_in, qa_norm, kv_norm, w_qb, q_norm, k_norm, fq_norm, fk_norm, f_bias):
    b, s, _ = h.shape
    u = h @ w_in
    cq, ckv, kr, fq, fk, fv, fl = jnp.split(u, AB_SPLITS, axis=-1)
    cq = rms_norm(cq, qa_norm)
    ckv = rms_norm(ckv, kv_norm)
    q = (cq @ w_qb).reshape(b, s, H_A, NOPE + ROPE_D)
    q_nope = rms_norm(q[..., :NOPE], q_norm[:NOPE])
    q_rope = rope(rms_norm(q[..., NOPE:], q_norm[NOPE:]), pos)
    k_rope = rope(rms_norm(kr, k_norm[NOPE:]), pos)
    fq = rms_norm(fq.reshape(b, s, H_B, HEAD_DIM), fq_norm)
    fk = rms_norm(fk.reshape(b, s, KV_B, HEAD_DIM), fk_norm)
    fv = fv.reshape(b, s, KV_B, HEAD_DIM)
    logf = jax.nn.log_sigmoid(fl.astype(jnp.float32) + f_bias.astype(jnp.float32))
    return q_nope, q_rope, ckv, k_rope, fq, jnp.stack([fk, fv], axis=2), logf


def ab_prompt(h, pos, w_in, qa_norm, kv_norm, w_qb, w_kvb, q_norm, k_norm, fq_norm, fk_norm, f_bias, w_out):
    b, s, _ = h.shape
    q_nope, q_rope, ckv, k_rope, fq, fkv, logf = ab_project(h, pos, w_in, qa_norm, kv_norm, w_qb, q_norm, k_norm, fq_norm, fk_norm, f_bias)
    kv = (ckv @ w_kvb).reshape(b, s, H_A, NOPE + V_DIM)
    k_nope = rms_norm(kv[..., :NOPE], k_norm[:NOPE])
    v_a = kv[..., NOPE:]
    fk, fv = fkv[:, :, 0], fkv[:, :, 1]
    cum_f = lax.cumsum(logf, axis=1)
    cum_f_keys = jnp.swapaxes(cum_f, 1, 2)[:, :, None, :]
    kpos = jnp.arange(s)

    def block(i, qn, qr, fqb, cfq):
        qpos = i * Q_BLOCK + jnp.arange(Q_BLOCK)
        mask = kpos[None, :] <= qpos[:, None]
        sa = (jnp.einsum('bqhd,bkhd->bhqk', qn, k_nope, preferred_element_type=jnp.float32)
              + jnp.einsum('bqhd,bkd->bhqk', qr, k_rope, preferred_element_type=jnp.float32)) * MLA_SCALE
        oa = jnp.einsum('bhqk,bkhd->bqhd', masked_softmax(sa, mask), v_a)
        sb = jnp.einsum('bqkgd,bskd->bkgqs', fqb.reshape(b, Q_BLOCK, KV_B, G_B, HEAD_DIM), fk, preferred_element_type=jnp.float32)
        sb = sb.reshape(b, H_B, Q_BLOCK, s) * HD_SCALE + jnp.swapaxes(cfq, 1, 2)[..., None] - cum_f_keys
        pb = masked_softmax(sb, mask).reshape(b, KV_B, G_B, Q_BLOCK, s)
        ob = jnp.einsum('bkgqs,bskd->bqkgd', pb, fv)
        return jnp.concatenate([oa.reshape(b, Q_BLOCK, H_A * V_DIM), ob.reshape(b, Q_BLOCK, H_B * HEAD_DIM)], axis=-1)

    o = sweep_query_blocks(block, (q_nope, q_rope, fq, cum_f), s)
    return o.astype(h.dtype) @ w_out, (ckv, k_rope, fkv, logf)


def ab_sample(h, pos, ckv_cache, krope_cache, fkv_cache, logf_cache, page_table,
              w_in, qa_norm, kv_norm, w_qb, w_kvb, q_norm, k_norm, fq_norm, fk_norm, f_bias, w_out):
    b, s, _ = h.shape
    q_nope, q_rope, ckv, k_rope, fq, fkv, logf = ab_project(h, pos, w_in, qa_norm, kv_norm, w_qb, q_norm, k_norm, fq_norm, fk_norm, f_bias)
    ckv_p = gather_pages(ckv_cache, page_table)
    kr_p = gather_pages(krope_cache, page_table)
    fkv_p = gather_pages(fkv_cache, page_table)
    lf_p = gather_pages(logf_cache, page_table)
    past = ckv_p.shape[1]
    mask = jnp.concatenate([jnp.ones((s, past), bool), jnp.tril(jnp.ones((s, s), bool))], axis=1)
    w_kvb3 = w_kvb.reshape(KV_LORA, H_A, NOPE + V_DIM)
    w_kb, w_vb = w_kvb3[..., :NOPE], w_kvb3[..., NOPE:]
    kn_p = rms_norm(jnp.einsum('bkr,rhd->bkhd', ckv_p, w_kb), k_norm[:NOPE])
    kn_n = rms_norm(jnp.einsum('bkr,rhd->bkhd', ckv, w_kb), k_norm[:NOPE])
    sa = jnp.concatenate([
        jnp.einsum('bqhd,bkhd->bhqk', q_nope, kn_p, preferred_element_type=jnp.float32)
        + jnp.einsum('bqhd,bkd->bhqk', q_rope, kr_p, preferred_element_type=jnp.float32),
        jnp.einsum('bqhd,bkhd->bhqk', q_nope, kn_n, preferred_element_type=jnp.float32)
        + jnp.einsum('bqhd,bkd->bhqk', q_rope, k_rope, preferred_element_type=jnp.float32)], axis=-1) * MLA_SCALE
    pa = masked_softmax(sa, mask)
    lat = jnp.einsum('bhqk,bkr->bqhr', pa[..., :past], ckv_p) + jnp.einsum('bhqk,bkr->bqhr', pa[..., past:], ckv)
    oa = jnp.einsum('bqhr,rhv->bqhv', lat, w_vb)
    cum_f = lax.cumsum(jnp.concatenate([lf_p.astype(jnp.float32), logf], axis=1), axis=1)
    qg = fq.reshape(b, s, KV_B, G_B, HEAD_DIM)
    sb = jnp.concatenate([
        jnp.einsum('bqkgd,bskd->bkgqs', qg, fkv_p[:, :, 0], preferred_element_type=jnp.float32),
        jnp.einsum('bqkgd,bskd->bkgqs', qg, fkv[:, :, 0], preferred_element_type=jnp.float32)], axis=-1)
    sb = (sb.reshape(b, H_B, s, past + s) * HD_SCALE
          + jnp.swapaxes(cum_f[:, past:], 1, 2)[..., None] - jnp.swapaxes(cum_f, 1, 2)[:, :, None, :])
    pb = masked_softmax(sb, mask).reshape(b, KV_B, G_B, s, past + s)
    ob = (jnp.einsum('bkgqs,bskd->bqkgd', pb[..., :past], fkv_p[:, :, 1])
          + jnp.einsum('bkgqs,bskd->bqkgd', pb[..., past:], fkv[:, :, 1]))
    o = jnp.concatenate([oa.reshape(b, s, H_A * V_DIM), ob.reshape(b, s, H_B * HEAD_DIM)], axis=-1)
    return o.astype(h.dtype) @ w_out, (ckv, k_rope, fkv, logf)


def dsa_project(h, pos, w_in, q_norm, k_norm):
    b, s, _ = h.shape
    u = (h @ w_in).reshape(b, s, N_C_GROUPS, 3, H_C, HEAD_DIM)
    q = rope(rms_norm(u[:, :, :, 0], q_norm), pos)
    k = rope(rms_norm(u[:, :, :, 1], k_norm), pos)
    return q, jnp.stack([k, u[:, :, :, 2]], axis=3)


def dilated_attend(q, kv_src, kidx, valid):
    kvg = kv_src[:, kidx]
    sc = jnp.einsum('bqhd,bqnhd->bqhn', q, kvg[:, :, :, 0], preferred_element_type=jnp.float32) * HD_SCALE
    sc = jnp.where(valid[None, :, None, :], sc, NEG_INF)
    mx = jnp.max(sc, axis=-1)
    e = jnp.exp(sc - mx[..., None])
    num = jnp.einsum('bqhn,bqnhd->bqhd', e, kvg[:, :, :, 1])
    return num, mx, jnp.sum(e, axis=-1)


def combine_groups(parts):
    mx = functools.reduce(jnp.maximum, [p[1] for p in parts])
    wts = [jnp.exp(p[1] - mx) for p in parts]
    num = sum(w[..., None] * p[0] for w, p in zip(wts, parts))
    den = sum(w * p[2] for w, p in zip(wts, parts))
    return num / den[..., None]


def dsa_prompt(h, pos, w_in, q_norm, k_norm, w_out):
    b, s, _ = h.shape
    q, kv = dsa_project(h, pos, w_in, q_norm, k_norm)
    kv_groups = [kv[:, :, g] for g in range(N_C_GROUPS)]

    def block(i, qb):
        qpos = i * Q_BLOCK + jnp.arange(Q_BLOCK)
        parts = []
        for g in range(N_C_GROUPS):
            offs = jnp.arange(0, C_WINDOWS[g] + 1, C_DILATIONS[g])
            kidx = qpos[:, None] - offs[None, :]
            parts.append(dilated_attend(qb[:, :, g], kv_groups[g], jnp.maximum(kidx, 0), kidx >= 0))
        return combine_groups(parts)

    o = sweep_query_blocks(block, (q,), s).reshape(b, s, OUT_C)
    bufs = [kv_groups[g][:, s - min(C_WINDOWS[g], s):] for g in range(N_C_GROUPS)]
    return o.astype(h.dtype) @ w_out, bufs


def dsa_sample(h, pos, bufs, w_in, q_norm, k_norm, w_out):
    b, s, _ = h.shape
    q, kv = dsa_project(h, pos, w_in, q_norm, k_norm)
    parts, new_bufs = [], []
    for g in range(N_C_GROUPS):
        buf = bufs[g]
        wb = buf.shape[1]
        ext = jnp.concatenate([buf.astype(kv.dtype), kv[:, :, g]], axis=1)
        offs = jnp.arange(0, C_WINDOWS[g] + 1, C_DILATIONS[g])
        kidx = wb + jnp.arange(s)[:, None] - offs[None, :]
        parts.append(dilated_attend(q[:, :, g], ext, jnp.maximum(kidx, 0), kidx >= 0))
        new_bufs.append(ext[:, wb + s - min(C_WINDOWS[g], wb + s):])
    o = combine_groups(parts).reshape(b, s, OUT_C)
    return o.astype(h.dtype) @ w_out, new_bufs


def moe_ffn(x, w_r, b_r, w1, b1, w2, b2):
    shape = x.shape
    x = x.reshape(-1, shape[-1])
    n, d = x.shape
    logits = (x @ w_r).astype(jnp.float32) + b_r.astype(jnp.float32)
    top_v, top_e = lax.top_k(logits, TOP_K)
    gates = jax.nn.softmax(top_v, axis=-1)
    m = n * TOP_K
    flat_e = top_e.reshape(m)
    flat_tok = jnp.arange(m) // TOP_K
    order = jnp.argsort(flat_e)
    se = flat_e[order]
    counts = jnp.bincount(flat_e, length=N_EXPERTS)
    padded = (counts + MOE_BLOCK - 1) // MOE_BLOCK * MOE_BLOCK
    pend = jnp.cumsum(padded)
    pstart = pend - padded
    start = jnp.cumsum(counts) - counts
    dest = pstart[se] + jnp.arange(m) - start[se]
    nblk = -(-m // MOE_BLOCK) + N_EXPERTS
    rows = nblk * MOE_BLOCK
    row_tok = jnp.full((rows,), n, jnp.int32).at[dest].set(flat_tok[order])
    row_g = jnp.zeros((rows,), jnp.float32).at[dest].set(gates.reshape(m)[order])
    blk_e = jnp.minimum(jnp.searchsorted(pend, jnp.arange(nblk) * MOE_BLOCK, side='right'), N_EXPERTS - 1)
    xp = jnp.concatenate([x, jnp.zeros((1, d), x.dtype)], axis=0)
    xb = xp[row_tok].reshape(nblk, MOE_BLOCK, d)

    def expert_block(args):
        xe, e = args
        hcat = xe @ w1[e] + b1[e]
        glu = jnp.minimum(hcat[..., :D_FF], SWIGLU_LIMIT)
        lin = jnp.clip(hcat[..., D_FF:], -SWIGLU_LIMIT, SWIGLU_LIMIT)
        act = glu * jax.nn.sigmoid(SWIGLU_ALPHA * glu) * (lin + 1)
        return act @ w2[e] + b2[e]

    yb = lax.map(expert_block, (xb, blk_e)).reshape(rows, d)
    out = jnp.zeros((n + 1, d), jnp.float32).at[row_tok].add(yb.astype(jnp.float32) * row_g[:, None])
    return out[:n].astype(x.dtype).reshape(shape)


def setup_inputs(seed: int = 0) -> dict:
    key = jax.random.key(seed)
    keys = list(jax.random.split(key, 48))
    f32 = jnp.float32

    def nrm(shape, scale=1.0):
        return scale * jax.random.normal(keys.pop(), shape, f32)

    def gain(shape):
        return 1.0 + 0.02 * jax.random.normal(keys.pop(), shape, f32)

    n_pages = PAST_LEN // PAGE_SIZE
    n_pool = (DEC_BATCH * n_pages * 5) // 4
    page_table = jax.random.permutation(keys.pop(), n_pool)[:DEC_BATCH * n_pages].reshape(DEC_BATCH, n_pages).astype(jnp.int32)
    cb = [min(w, PAST_LEN) for w in C_WINDOWS]
    na, nc = N_A_LAYERS, N_C_LAYERS
    return {
        'x_prompt': nrm((BATCH, SEQ, D_MODEL)),
        'x_sample': nrm((DEC_BATCH, DEC_SEQ, D_MODEL)),
        'cache_mla_ckv': nrm((na, n_pool, PAGE_SIZE, KV_LORA)),
        'cache_mla_krope': nrm((na, n_pool, PAGE_SIZE, ROPE_D)),
        'cache_fox_kv': nrm((na, n_pool, PAGE_SIZE, 2, KV_B, HEAD_DIM)),
        'cache_fox_logf': jax.nn.log_sigmoid(nrm((na, n_pool, PAGE_SIZE, H_B)) + 2.0),
        'state_c1_kv': nrm((nc, DEC_BATCH, cb[0], 2, H_C, HEAD_DIM)),
        'state_c2_kv': nrm((nc, DEC_BATCH, cb[1], 2, H_C, HEAD_DIM)),
        'state_c3_kv': nrm((nc, DEC_BATCH, cb[2], 2, H_C, HEAD_DIM)),
        'page_table': page_table,
        'c_prompt': nrm((BATCH, D_MODEL)),
        'c_sample': nrm((DEC_BATCH, D_MODEL)),
        'ada_w': nrm((DEPTH, D_MODEL, 6 * D_MODEL), 0.5 * D_MODEL ** -0.5),
        'ada_b': nrm((DEPTH, 6 * D_MODEL), 0.02),
        'norm1_g': gain((DEPTH, D_MODEL)),
        'norm2_g': gain((DEPTH, D_MODEL)),
        'ab_w_in': nrm((na, D_MODEL, IN_AB), D_MODEL ** -0.5),
        'mla_qa_norm': gain((na, Q_LORA)),
        'mla_kv_norm': gain((na, KV_LORA)),
        'mla_w_qb': nrm((na, Q_LORA, H_A * (NOPE + ROPE_D)), Q_LORA ** -0.5),
        'mla_w_kvb': nrm((na, KV_LORA, H_A * (NOPE + V_DIM)), KV_LORA ** -0.5),
        'mla_q_norm': gain((na, NOPE + ROPE_D)),
        'mla_k_norm': gain((na, NOPE + ROPE_D)),
        'fox_q_norm': gain((na, HEAD_DIM)),
        'fox_k_norm': gain((na, HEAD_DIM)),
        'fox_f_bias': nrm((na, H_B), 0.1) + 2.0,
        'ab_w_out': nrm((na, OUT_AB, D_MODEL), OUT_AB ** -0.5),
        'c_w_in': nrm((nc, D_MODEL, IN_C), D_MODEL ** -0.5),
        'c_q_norm': gain((nc, HEAD_DIM)),
        'c_k_norm': gain((nc, HEAD_DIM)),
        'c_w_out': nrm((nc, OUT_C, D_MODEL), OUT_C ** -0.5),
        'router_w': nrm((DEPTH, D_MODEL, N_EXPERTS), D_MODEL ** -0.5),
        'router_b': nrm((DEPTH, N_EXPERTS), 0.01),
        'moe_w1': nrm((DEPTH, N_EXPERTS, D_MODEL, 2 * D_FF), D_MODEL ** -0.5),
        'moe_b1': nrm((DEPTH, N_EXPERTS, 2 * D_FF), 0.02),
        'moe_w2': nrm((DEPTH, N_EXPERTS, D_FF, D_MODEL), D_FF ** -0.5),
        'moe_b2': nrm((DEPTH, N_EXPERTS, D_MODEL), 0.02),
    }


def reference(x_prompt, x_sample, cache_mla_ckv, cache_mla_krope, cache_fox_kv, cache_fox_logf,
              state_c1_kv, state_c2_kv, state_c3_kv, page_table, c_prompt, c_sample,
              ada_w, ada_b, norm1_g, norm2_g, ab_w_in, mla_qa_norm, mla_kv_norm, mla_w_qb, mla_w_kvb,
              mla_q_norm, mla_k_norm, fox_q_norm, fox_k_norm, fox_f_bias, ab_w_out,
              c_w_in, c_q_norm, c_k_norm, c_w_out, router_w, router_b, moe_w1, moe_b1, moe_w2, moe_b2):
    past = page_table.shape[1] * PAGE_SIZE
    pos_p = jnp.arange(x_prompt.shape[1], dtype=jnp.int32)
    pos_s = past + jnp.arange(x_sample.shape[1], dtype=jnp.int32)
    hp, hs = x_prompt, x_sample
    new_a_p, new_a_s, new_c_p, new_c_s = [], [], [], []
    for layer in range(DEPTH):
        mp = adaln(c_prompt, ada_w[layer], ada_b[layer])
        ms = adaln(c_sample, ada_w[layer], ada_b[layer])
        up = modulate(hp, norm1_g[layer], mp[0], mp[1])
        us = modulate(hs, norm1_g[layer], ms[0], ms[1])
        j = layer // 2
        if layer % 2 == 0:
            prm = (ab_w_in[j], mla_qa_norm[j], mla_kv_norm[j], mla_w_qb[j], mla_w_kvb[j], mla_q_norm[j],
                   mla_k_norm[j], fox_q_norm[j], fox_k_norm[j], fox_f_bias[j], ab_w_out[j])
            op, st_p = ab_prompt(up, pos_p, *prm)
            osm, st_s = ab_sample(us, pos_s, cache_mla_ckv[j], cache_mla_krope[j], cache_fox_kv[j],
                                  cache_fox_logf[j], page_table, *prm)
            new_a_p.append(st_p)
            new_a_s.append(st_s)
        else:
            op, st_p = dsa_prompt(up, pos_p, c_w_in[j], c_q_norm[j], c_k_norm[j], c_w_out[j])
            osm, st_s = dsa_sample(us, pos_s, (state_c1_kv[j], state_c2_kv[j], state_c3_kv[j]),
                                   c_w_in[j], c_q_norm[j], c_k_norm[j], c_w_out[j])
            new_c_p.append(st_p)
            new_c_s.append(st_s)
        hp = hp + mp[2] * op
        hs = hs + ms[2] * osm
        moe_args = (router_w[layer], router_b[layer], moe_w1[layer], moe_b1[layer], moe_w2[layer], moe_b2[layer])
        hp = hp + mp[5] * moe_ffn(modulate(hp, norm2_g[layer], mp[3], mp[4]), *moe_args)
        hs = hs + ms[5] * moe_ffn(modulate(hs, norm2_g[layer], ms[3], ms[4]), *moe_args)

    def stack(entries, i):
        return jnp.stack([e[i] for e in entries], axis=0)

    return (hp, hs,
            stack(new_a_p, 0), stack(new_a_p, 1), stack(new_a_p, 2), stack(new_a_p, 3),
            stack(new_c_p, 0), stack(new_c_p, 1), stack(new_c_p, 2),
            stack(new_a_s, 0), stack(new_a_s, 1), stack(new_a_s, 2), stack(new_a_s, 3),
            stack(new_c_s, 0), stack(new_c_s, 1), stack(new_c_s, 2))
```

```python
import functools

import numpy as np
import jax
import jax.numpy as jnp
from jax import lax
from jax.experimental import pallas as pl
from jax.experimental.pallas import tpu as pltpu

F32 = jnp.float32
BF16 = jnp.bfloat16
I32 = jnp.int32

D_MODEL = 1024
PAGE = 128
HEAD_DIM = 64
H_A = 8
Q_LORA = 384
KV_LORA = 256
NOPE = 64
ROPE_D = 32
V_DIM = 64
H_B = 8
KV_B = 4
H_C = 8
C_WINDOWS = (128, 512, 2048)
C_DILATIONS = (1, 4, 16)
N_EXPERTS = 32
TOP_K = 4
D_FF = 1024
SWIGLU_ALPHA = 1.702
SWIGLU_LIMIT = 7.0
ROPE_THETA = 10000.0
NORM_EPS = 1e-6
NEG = -1e30
MLA_SCALE = (NOPE + ROPE_D) ** -0.5
HD_SCALE = HEAD_DIM ** -0.5

LANES = 128
SUBLANES = 8
VMEM_LIMIT = 56 * 1024 * 1024
ROW_TILE_CHUNKS = D_MODEL // LANES


def _cparams(sem):
    return pltpu.CompilerParams(dimension_semantics=sem, vmem_limit_bytes=VMEM_LIMIT)


def _dot(a, b):
    return jnp.dot(a.astype(BF16), b.astype(BF16), preferred_element_type=F32)


def _dot_nt(a, b):
    return lax.dot_general(a.astype(BF16), b.astype(BF16), (((1,), (1,)), ((), ())),
                           preferred_element_type=F32)


def _split3(x):
    hi = x.astype(BF16)
    r = x - hi.astype(F32)
    mid = r.astype(BF16)
    lo = (r - mid.astype(F32)).astype(BF16)
    return hi, mid, lo


def _dot3(a, b):
    hi, mid, lo = _split3(a)
    return (jnp.dot(hi, b, preferred_element_type=F32) + jnp.dot(mid, b, preferred_element_type=F32)
            + jnp.dot(lo, b, preferred_element_type=F32))


def _modulate(x, g, scale, shift):
    ms = jnp.mean(x * x, axis=-1, keepdims=True)
    return (x * lax.rsqrt(ms + NORM_EPS) * g) * (1.0 + scale) + shift


def _group_rms(x, s_blk, inv_cnt):
    sq = (x * x).astype(BF16)
    parts = [jnp.dot(sq[:, c * LANES:(c + 1) * LANES], s_blk, preferred_element_type=F32)
             for c in range(x.shape[1] // LANES)]
    ssq = parts[0] if len(parts) == 1 else jnp.concatenate(parts, axis=1)
    return lax.rsqrt(ssq * inv_cnt + NORM_EPS)


def _tile_lanes(x, n):
    return jnp.concatenate([x] * n, axis=1)


def _lane_iota(shape):
    return lax.broadcasted_iota(I32, shape, len(shape) - 1)


def _adaln_kernel(c_ref, w_ref, b_ref, o_ref):
    c = c_ref[...]
    s = c * jax.nn.sigmoid(c)
    o_ref[0] = _dot(s, w_ref[0]) + b_ref[0]


def _adaln(c_all, ada_w, ada_b):
    depth, d, n6 = ada_w.shape
    r = c_all.shape[0]
    tn = 768
    return pl.pallas_call(
        _adaln_kernel,
        grid=(depth, n6 // tn),
        in_specs=[pl.BlockSpec((r, d), lambda l, j: (0, 0)),
                  pl.BlockSpec((1, d, tn), lambda l, j: (l, 0, j)),
                  pl.BlockSpec((1, 1, tn), lambda l, j: (l, 0, j))],
        out_specs=pl.BlockSpec((1, r, tn), lambda l, j: (l, 0, j)),
        out_shape=jax.ShapeDtypeStruct((depth, r, n6), F32),
        compiler_params=_cparams(("arbitrary", "arbitrary")),
        name="adaln",
    )(c_all, ada_w, ada_b.reshape(depth, 1, n6))


def _mod_spec(arr, tiles_per_group):
    g, r, d = arr.shape
    if g == 1:
        return pl.BlockSpec((1, r, d), lambda i: (0, 0, 0))
    return pl.BlockSpec((1, r, d), lambda i: (i // tiles_per_group, 0, 0))


def _const_spec(arr):
    nd = arr.ndim
    return pl.BlockSpec(arr.shape, lambda *_: (0,) * nd)


def _rope_tables(pos, half, layout_fn):
    inv_freq = ROPE_THETA ** (-jnp.arange(half, dtype=F32) / half)
    ang = pos.astype(F32)[:, None] * inv_freq[None, :]
    return layout_fn(jnp.cos(ang)), layout_fn(jnp.sin(ang))


def _l0_consts(w_in, qa_norm, kv_norm, w_qb, w_kvb, q_norm, k_norm, fq_norm, fk_norm, f_bias):
    d = w_in.shape[0]
    z = lambda n: jnp.zeros((d, n), F32)
    cq, ckv = w_in[:, :Q_LORA], w_in[:, Q_LORA:Q_LORA + KV_LORA]
    o = Q_LORA + KV_LORA
    kr = w_in[:, o:o + ROPE_D]
    o += ROPE_D
    fq = w_in[:, o:o + H_B * HEAD_DIM]
    o += H_B * HEAD_DIM
    fk = w_in[:, o:o + KV_B * HEAD_DIM]
    o += KV_B * HEAD_DIM
    fv = w_in[:, o:o + KV_B * HEAD_DIM]
    o += KV_B * HEAD_DIM
    fl = w_in[:, o:o + H_B]
    hr = ROPE_D // 2
    krr = jnp.concatenate([-kr[:, hr:], kr[:, :hr]], axis=1)
    g1 = jnp.concatenate([kr, z(32), kr, z(32)], axis=1)
    g2 = jnp.concatenate([krr, z(32), krr, z(32)], axis=1)
    w_in2 = jnp.concatenate([cq, ckv, g1, g2, fq, fk, fv, fl, z(LANES - H_B)], axis=1).astype(BF16)

    wq = w_qb.reshape(Q_LORA, H_A, NOPE + ROPE_D)
    zq = lambda n: jnp.zeros((Q_LORA, H_A, n), F32)
    rope_c = wq[:, :, NOPE:]
    rope_r = jnp.concatenate([-rope_c[:, :, hr:], rope_c[:, :, :hr]], axis=2)
    wqa = jnp.concatenate([wq[:, :, :NOPE], rope_c, zq(32)], axis=2).reshape(Q_LORA, H_A * LANES)
    wqb = jnp.concatenate([zq(NOPE), rope_r, zq(32)], axis=2).reshape(Q_LORA, H_A * LANES)
    w_q2 = jnp.concatenate([wqa, wqb], axis=1).astype(BF16)

    wkv = w_kvb.reshape(KV_LORA, H_A, NOPE + V_DIM)
    wk = jnp.concatenate([wkv[:, :, :NOPE], jnp.zeros((KV_LORA, H_A, LANES - NOPE), F32)], axis=2)
    w_kv2 = jnp.concatenate([wk.reshape(KV_LORA, H_A * LANES),
                             wkv[:, :, NOPE:].reshape(KV_LORA, H_A * V_DIM)], axis=1).astype(BF16)

    z32 = jnp.zeros((32,), F32)
    qn_r = q_norm[NOPE:]
    ga = jnp.tile(jnp.concatenate([q_norm[:NOPE], qn_r, z32]), H_A) * MLA_SCALE
    gb = jnp.tile(jnp.concatenate([jnp.zeros((NOPE,), F32), qn_r[hr:], qn_r[:hr], z32]), H_A) * MLA_SCALE
    gk = jnp.tile(jnp.concatenate([k_norm[:NOPE], jnp.zeros((LANES - NOPE,), F32)]), H_A)
    kn_r = k_norm[NOPE:]
    kn_rr = jnp.concatenate([kn_r[hr:], kn_r[:hr]])
    gk1 = jnp.concatenate([kn_r, z32, kn_r, z32])
    gk2 = jnp.concatenate([kn_rr, z32, kn_rr, z32])
    vecs = dict(
        qa_norm=qa_norm[None], kv_norm=kv_norm[None], ga=ga[None], gb=gb[None], gk=gk[None],
        gk1=gk1[None], gk2=gk2[None],
        gfq=(jnp.tile(fq_norm, H_B) * HD_SCALE)[None], gfk=jnp.tile(fk_norm, KV_B)[None],
        fbias=jnp.concatenate([f_bias, jnp.zeros((LANES - H_B,), F32)])[None],
    )

    li = np.arange(LANES)
    sq = ((li[:, None] < 64) & (li[None, :] < 64)) | ((li[:, None] >= 64) & (li[:, None] < 96)
                                                       & (li[None, :] >= 64) & (li[None, :] < 96))
    sk = (li[:, None] < 64) & (li[None, :] < 64)
    sf = (li[:, None] // 64) == (li[None, :] // 64)
    cnt_q = np.where(li < 64, 1.0 / 64, np.where(li < 96, 1.0 / 32, 1.0)).astype(np.float32)
    mats = dict(
        s_q=jnp.asarray(sq, BF16), s_k=jnp.asarray(sk, BF16), s_f=jnp.asarray(sf, BF16),
        cnt_q=jnp.asarray(np.tile(cnt_q, H_A))[None],
    )
    return w_in2, w_q2, w_kv2, vecs, mats


def _l0_common(u, wq2_ref, wkv2_ref, p, cq_t, sq_t, ck_t, sk_t):
    cq = u[:, :Q_LORA]
    ckv = u[:, Q_LORA:640]
    g1 = u[:, 640:768]
    g2 = u[:, 768:896]
    fq = u[:, 896:1408]
    fk = u[:, 1408:1664]
    fv = u[:, 1664:1920]
    fl = u[:, 1920:2048]

    cq_n = cq * lax.rsqrt(jnp.mean(cq * cq, axis=-1, keepdims=True) + NORM_EPS) * p["qa_norm"][...]
    q2 = _dot(cq_n, wq2_ref[...])
    qa, qb = q2[:, :H_A * LANES], q2[:, H_A * LANES:]
    rq = _group_rms(qa, p["s_q"][...], p["cnt_q"][...])
    q_mla = rq * (qa * p["ga"][...] * _tile_lanes(cq_t, H_A) + qb * p["gb"][...] * _tile_lanes(sq_t, H_A))

    ckv_n = ckv * lax.rsqrt(jnp.mean(ckv * ckv, axis=-1, keepdims=True) + NORM_EPS) * p["kv_norm"][...]
    kv2 = _dot(ckv_n, wkv2_ref[...])
    kk, v_mla = kv2[:, :H_A * LANES], kv2[:, H_A * LANES:]
    rk = _group_rms(kk, p["s_k"][...], 1.0 / NOPE)
    k_nope = kk * rk * p["gk"][...]

    lane = _lane_iota(g1.shape)
    ss = jnp.sum(jnp.where(lane < ROPE_D, g1 * g1, 0.0), axis=-1, keepdims=True)
    r_kr = lax.rsqrt(ss * (1.0 / ROPE_D) + NORM_EPS)
    kr128 = r_kr * (g1 * p["gk1"][...] * ck_t + g2 * p["gk2"][...] * sk_t)

    rfq = _group_rms(fq, p["s_f"][...], 1.0 / HEAD_DIM)
    fq_n = fq * rfq * p["gfq"][...]
    rfk = _group_rms(fk, p["s_f"][...], 1.0 / HEAD_DIM)
    fk_n = fk * rfk * p["gfk"][...]
    xl = fl + p["fbias"][...]
    logf = jnp.minimum(xl, 0.0) - jnp.log(1.0 + jnp.exp(-jnp.abs(xl)))
    return q_mla, ckv_n, k_nope, v_mla, kr128, fq_n, fk_n, fv, logf


_L0_VEC_NAMES = ("qa_norm", "kv_norm", "ga", "gb", "gk", "gk1", "gk2", "gfq", "gfk", "fbias")
_L0_MAT_NAMES = ("s_q", "s_k", "s_f", "cnt_q")


def _l0_prompt_kernel(tiles_per_seq, x_ref, shift_ref, scale_ref, g_ref, win_ref, wq2_ref, wkv2_ref,
                      cq_ref, sq_ref, ck_ref, sk_ref, eq_ref, ek_ref, ev_ref, pcf_ref, ltri_ref, ones_ref,
                      *rest):
    nv, nm = len(_L0_VEC_NAMES), len(_L0_MAT_NAMES)
    p = dict(zip(_L0_VEC_NAMES + _L0_MAT_NAMES, rest[:nv + nm]))
    q_out, k_out, v_out, ckv_out, kr_out, fkv_out, lf_out, carry = rest[nv + nm:]

    @pl.when(pl.program_id(0) % tiles_per_seq == 0)
    def _():
        carry[...] = jnp.zeros_like(carry)

    xm = _modulate(x_ref[...], g_ref[...], scale_ref[0], shift_ref[0])
    u = _dot(xm, win_ref[...])
    q_mla, ckv_n, k_nope, v_mla, kr128, fq_n, fk_n, fv, logf = _l0_common(
        u, wq2_ref, wkv2_ref, p, cq_ref[...], sq_ref[...], ck_ref[...], sk_ref[...])

    lane = _lane_iota(kr128.shape)
    k_mla = k_nope + _tile_lanes(jnp.where(lane >= NOPE, kr128, 0.0), H_A)

    ltri = ltri_ref[...]
    cf = _dot3_left(ltri, logf) + carry[...]
    carry[...] = cf[cf.shape[0] - 1:, :]
    nh, nm_, nl = _split3(-cf)
    bias = jnp.dot(jnp.concatenate([nh, nm_, nl], axis=1), pcf_ref[...], preferred_element_type=F32)

    q_fox = _dot(fq_n, eq_ref[...]) + ones_ref[...]
    k_fox = _dot(fk_n, ek_ref[...]) + bias
    v_fox = _dot(fv, ev_ref[...])

    q_out[...] = jnp.concatenate([q_mla, q_fox], axis=1).astype(BF16)
    k_out[...] = jnp.concatenate([k_mla, k_fox], axis=1).astype(BF16)
    v_out[...] = jnp.concatenate([v_mla, v_fox], axis=1).astype(BF16)
    ckv_out[...] = ckv_n
    kr_out[...] = kr128[:, :ROPE_D]
    fkv_out[...] = jnp.concatenate([fk_n, fv], axis=1)
    lf_out[...] = logf[:, :H_B]


def _dot3_left(m01, x):
    hi, mid, lo = _split3(x)
    return (jnp.dot(m01, hi, preferred_element_type=F32) + jnp.dot(m01, mid, preferred_element_type=F32)
            + jnp.dot(m01, lo, preferred_element_type=F32))


def _fox_place_mats():
    eq = np.zeros((H_B * HEAD_DIM, H_B * LANES), np.float32)
    ek = np.zeros((KV_B * HEAD_DIM, H_B * LANES), np.float32)
    ev = np.zeros((KV_B * HEAD_DIM, KV_B * LANES), np.float32)
    pcf = np.zeros((3 * LANES, H_B * LANES), np.float32)
    ones = np.zeros((1, H_B * LANES), np.float32)
    dd = np.arange(HEAD_DIM)
    for h in range(H_B):
        eq[h * HEAD_DIM + dd, h * LANES + dd] = 1.0
        ek[(h // 2) * HEAD_DIM + dd, h * LANES + dd] = 1.0
        for s in range(3):
            pcf[s * LANES + h, h * LANES + HEAD_DIM + s] = 1.0
            ones[0, h * LANES + HEAD_DIM + s] = 1.0
    for kvh in range(KV_B):
        for g in range(2):
            ev[kvh * HEAD_DIM + dd, kvh * LANES + g * HEAD_DIM + dd] = 1.0
    return (jnp.asarray(eq, BF16), jnp.asarray(ek, BF16), jnp.asarray(ev, BF16), jnp.asarray(pcf, BF16),
            jnp.asarray(ones, F32))


def _l0_tables(pos):
    def q_layout(t):
        n = t.shape[0]
        tt = jnp.concatenate([t, t], axis=1)
        return tt, n
    inv_freq = ROPE_THETA ** (-jnp.arange(ROPE_D // 2, dtype=F32) / (ROPE_D // 2))
    ang = pos.astype(F32)[:, None] * inv_freq[None, :]
    c = jnp.concatenate([jnp.cos(ang)] * 2, axis=1)
    s = jnp.concatenate([jnp.sin(ang)] * 2, axis=1)
    n = pos.shape[0]
    one = jnp.ones((n, NOPE), F32)
    z32 = jnp.zeros((n, 32), F32)
    z64 = jnp.zeros((n, NOPE), F32)
    cq = jnp.concatenate([one, c, z32], axis=1)
    sq = jnp.concatenate([z64, s, z32], axis=1)
    ck = jnp.concatenate([c, z32, c, z32], axis=1)
    sk = jnp.concatenate([s, z32, s, z32], axis=1)
    return cq, sq, ck, sk


def _l0_prompt_proj(x, shift, scale, g, consts, seq):
    n, d = x.shape
    tm = 256
    tps = seq // tm
    w_in2, w_q2, w_kv2, vecs, mats = consts
    cq, sq, ck, sk = _l0_tables(jnp.arange(seq, dtype=I32))
    eq, ek, ev, pcf, ones = _fox_place_mats()
    ltri = jnp.asarray(np.tril(np.ones((tm, tm), np.float32)), BF16)
    tab_spec = pl.BlockSpec((tm, LANES), lambda i: (i % tps, 0))
    row = lambda w: pl.BlockSpec((tm, w), lambda i: (i, 0))
    small = [vecs[k] for k in _L0_VEC_NAMES] + [mats[k] for k in _L0_MAT_NAMES]
    ins = [x, shift, scale, g, w_in2, w_q2, w_kv2, cq, sq, ck, sk, eq, ek, ev, pcf, ltri, ones] + small
    in_specs = ([row(d), _mod_spec(shift, tps), _mod_spec(scale, tps), _const_spec(g), _const_spec(w_in2),
                 _const_spec(w_q2), _const_spec(w_kv2), tab_spec, tab_spec, tab_spec, tab_spec,
                 _const_spec(eq), _const_spec(ek), _const_spec(ev), _const_spec(pcf), _const_spec(ltri),
                 _const_spec(ones)] + [_const_spec(a) for a in small])
    widths = (2 * H_A * LANES, 2 * H_A * LANES, H_A * V_DIM + KV_B * LANES, KV_LORA, ROPE_D,
              2 * KV_B * HEAD_DIM, H_B)
    dtypes = (BF16, BF16, BF16, F32, F32, F32, F32)
    return pl.pallas_call(
        functools.partial(_l0_prompt_kernel, tps),
        grid=(n // tm,),
        in_specs=in_specs,
        out_specs=[row(w) for w in widths],
        out_shape=[jax.ShapeDtypeStruct((n, w), dt) for w, dt in zip(widths, dtypes)],
        scratch_shapes=[pltpu.VMEM((1, LANES), F32)],
        compiler_params=_cparams(("arbitrary",)),
        name="l0_prompt_proj",
    )(*ins)


def _l0_sample_kernel(x_ref, shift_ref, scale_ref, g_ref, win_ref, wq2_ref, wkv2_ref,
                      cq_ref, sq_ref, ck_ref, sk_ref, *rest):
    nv, nm = len(_L0_VEC_NAMES), len(_L0_MAT_NAMES)
    p = dict(zip(_L0_VEC_NAMES + _L0_MAT_NAMES, rest[:nv + nm]))
    q_out, ckv_out, kr_out, fq_out, fkv_out, lf_out = rest[nv + nm:]
    xm = _modulate(x_ref[...], g_ref[...], scale_ref[0], shift_ref[0])
    u = _dot(xm, win_ref[...])
    q_mla, ckv_n, _, _, kr128, fq_n, fk_n, fv, logf = _l0_common(
        u, wq2_ref, wkv2_ref, p, cq_ref[...], sq_ref[...], ck_ref[...], sk_ref[...])
    q_out[...] = q_mla
    ckv_out[...] = ckv_n
    kr_out[...] = kr128
    fq_out[...] = fq_n
    fkv_out[...] = jnp.concatenate([fk_n, fv], axis=1)
    lf_out[...] = logf


def _l0_sample_proj(x, shift, scale, g, consts, past):
    n, d = x.shape
    w_in2, w_q2, w_kv2, vecs, mats = consts
    tabs = _l0_tables(jnp.full((1,), past, I32))
    small = [vecs[k] for k in _L0_VEC_NAMES] + [mats[k] for k in _L0_MAT_NAMES]
    ins = [x, shift, scale, g, w_in2, w_q2, w_kv2, *tabs] + small
    widths = (H_A * LANES, KV_LORA, LANES, H_B * HEAD_DIM, 2 * KV_B * HEAD_DIM, LANES)
    return pl.pallas_call(
        _l0_sample_kernel,
        grid=(1,),
        in_specs=[_const_spec(a) for a in ins],
        out_specs=[pl.BlockSpec((n, w), lambda i: (0, 0)) for w in widths],
        out_shape=[jax.ShapeDtypeStruct((n, w), F32) for w in widths],
        compiler_params=_cparams(("arbitrary",)),
        name="l0_sample_proj",
    )(*ins)


def _flash_kernel(tq, tk, q_ref, k_ref, v_ref, o_ref, m_scr, l_scr, acc_scr):
    i, j = pl.program_id(2), pl.program_id(3)

    @pl.when(j == 0)
    def _():
        m_scr[...] = jnp.full(m_scr.shape, NEG, F32)
        l_scr[...] = jnp.zeros_like(l_scr)
        acc_scr[...] = jnp.zeros_like(acc_scr)

    def step(masked):
        q = q_ref[0]
        k = k_ref[0]
        v = v_ref[0]
        lane = _lane_iota((tq, LANES))
        new_acc = []
        for h in range(2):
            s = _dot_nt(q[:, h * LANES:(h + 1) * LANES], k[:, h * LANES:(h + 1) * LANES])
            if masked:
                row = lax.broadcasted_iota(I32, (tq, tk), 0)
                col = lax.broadcasted_iota(I32, (tq, tk), 1)
                s = jnp.where(col <= row, s, NEG)
            m_prev = m_scr[h]
            m_new = jnp.maximum(m_prev, jnp.max(s, axis=-1, keepdims=True))
            alpha = jnp.exp(m_prev - m_new)
            pr = jnp.exp(s - m_new[:, :1])
            l_scr[h] = alpha * l_scr[h] + jnp.sum(pr, axis=-1, keepdims=True)
            m_scr[h] = m_new
            new_acc.append(alpha * acc_scr[...] + _dot(pr, v))
        acc_scr[...] = jnp.where(lane < V_DIM, new_acc[0], new_acc[1])

    @pl.when(j < i)
    def _():
        step(False)

    @pl.when(j == i)
    def _():
        step(True)
        lane = _lane_iota((tq, LANES))
        l = jnp.where(lane < V_DIM, l_scr[0], l_scr[1])
        o_ref[0] = (acc_scr[...] / l).astype(o_ref.dtype)


def _flash_attention(q, k, v):
    b, s, _ = q.shape
    tq = tk = 512 if s % 512 == 0 else 128
    npair = H_A // 2 + H_B // 2
    return pl.pallas_call(
        functools.partial(_flash_kernel, tq, tk),
        grid=(b, npair, s // tq, s // tk),
        in_specs=[pl.BlockSpec((1, tq, 2 * LANES), lambda bb, p, i, j: (bb, i, p)),
                  pl.BlockSpec((1, tk, 2 * LANES), lambda bb, p, i, j: (bb, jnp.minimum(j, i), p)),
                  pl.BlockSpec((1, tk, LANES), lambda bb, p, i, j: (bb, jnp.minimum(j, i), p))],
        out_specs=pl.BlockSpec((1, tq, LANES), lambda bb, p, i, j: (bb, i, p)),
        out_shape=jax.ShapeDtypeStruct((b, s, npair * LANES), BF16),
        scratch_shapes=[pltpu.VMEM((2, tq, LANES), F32), pltpu.VMEM((2, tq, LANES), F32),
                        pltpu.VMEM((tq, LANES), F32)],
        compiler_params=_cparams(("arbitrary",) * 4),
        name="l0_flash",
    )(q, k, v)


def _router_epilogue(h1, g2, shift2, scale2, wr_ref, br_ref, xm_ref, gate_ref, eidx_ref):
    tm = h1.shape[0]
    xm = _modulate(h1, g2, scale2, shift2)
    for c in range(ROW_TILE_CHUNKS):
        xm_ref[pl.ds(c, tm, stride=SUBLANES), :] = xm[:, c * LANES:(c + 1) * LANES]
    logits = jnp.dot(xm, wr_ref[...], preferred_element_type=F32, precision=lax.Precision.HIGHEST) + br_ref[...]
    lane = _lane_iota(logits.shape)
    x = logits
    vals, ev = [], jnp.zeros(logits.shape, I32)
    for kk in range(TOP_K):
        m = jnp.max(x, axis=-1, keepdims=True)
        idx = jnp.min(jnp.where(x == m, lane, LANES), axis=-1, keepdims=True)
        vals.append(m)
        ev = jnp.where(lane == kk, idx, ev)
        x = jnp.where(lane == idx, -3e38, x)
    es = [jnp.exp(vv - vals[0]) for vv in vals]
    tot = es[0] + es[1] + es[2] + es[3]
    gv = jnp.zeros(logits.shape, F32)
    for kk in range(TOP_K):
        gv = jnp.where(lane == kk, es[kk] / tot, gv)
    gate_ref[...] = gv
    eidx_ref[...] = ev


def _out_router_kernel(a_ref, wo_ref, res_ref, gate1_ref, g2_ref, shift2_ref, scale2_ref, wr_ref, br_ref,
                       h1_ref, xm_ref, gate_ref, eidx_ref):
    h1 = res_ref[...] + gate1_ref[0] * _dot(a_ref[...], wo_ref[...])
    h1_ref[...] = h1
    _router_epilogue(h1, g2_ref[...], shift2_ref[0], scale2_ref[0], wr_ref, br_ref, xm_ref, gate_ref, eidx_ref)


def _router_consts(router_w, router_b):
    d, e = router_w.shape
    wr = jnp.concatenate([router_w, jnp.zeros((d, LANES - e), F32)], axis=1)
    br = jnp.concatenate([router_b, jnp.full((LANES - e,), NEG, F32)])[None]
    return wr, br


def _router_out_specs(n, tm):
    specs = [pl.BlockSpec((tm, D_MODEL), lambda i: (i, 0)),
             pl.BlockSpec((tm * SUBLANES, LANES), lambda i: (i, 0)),
             pl.BlockSpec((tm, LANES), lambda i: (i, 0)),
             pl.BlockSpec((tm, LANES), lambda i: (i, 0))]
    shapes = [jax.ShapeDtypeStruct((n, D_MODEL), F32), jax.ShapeDtypeStruct((n * SUBLANES, LANES), F32),
              jax.ShapeDtypeStruct((n, LANES), F32), jax.ShapeDtypeStruct((n, LANES), I32)]
    return specs, shapes


def _out_router(a, w_out, res, gate1, g2, shift2, scale2, wr, br, rows_per_group):
    n, ka = a.shape
    tm = min(256, n)
    tpg = max(rows_per_group // tm, 1)
    wo = w_out.astype(BF16)
    out_specs, out_shapes = _router_out_specs(n, tm)
    return pl.pallas_call(
        _out_router_kernel,
        grid=(n // tm,),
        in_specs=[pl.BlockSpec((tm, ka), lambda i: (i, 0)), _const_spec(wo),
                  pl.BlockSpec((tm, D_MODEL), lambda i: (i, 0)), _mod_spec(gate1, tpg), _const_spec(g2),
                  _mod_spec(shift2, tpg), _mod_spec(scale2, tpg), _const_spec(wr), _const_spec(br)],
        out_specs=out_specs,
        out_shape=out_shapes,
        compiler_params=_cparams(("arbitrary",)),
        name="out_router",
    )(a, wo, res, gate1, g2, shift2, scale2, wr, br)


def _moe_plan(eidx, tb):
    n = eidx.shape[0]
    m = n * TOP_K
    flat_e = eidx[:, :TOP_K].reshape(m)
    onehot = (flat_e[:, None] == jnp.arange(N_EXPERTS, dtype=I32)[None, :]).astype(I32)
    csum = jnp.cumsum(onehot, axis=0)
    rank = jnp.take_along_axis(csum, flat_e[:, None], axis=1)[:, 0] - 1
    counts = csum[-1]
    padded = (counts + tb - 1) // tb * tb
    pend = jnp.cumsum(padded)
    pstart = pend - padded
    pos = pstart[flat_e] + rank
    nblk = -(-m // tb) + N_EXPERTS
    blk_e = jnp.minimum(jnp.searchsorted(pend, jnp.arange(nblk, dtype=I32) * tb, side="right"),
                        N_EXPERTS - 1).astype(I32)
    row_tok = jnp.zeros((nblk * tb,), I32).at[pos].set(jnp.arange(m, dtype=I32) // TOP_K)
    return pos.astype(I32), blk_e, row_tok, nblk


def _expert_kernel(tb, blk_e_ref, tok_ref, x_hbm, w1_ref, b1_ref, w2_ref, b2_ref, y_ref, xg, sem):
    del blk_e_ref

    def row_copy(r):
        tok = tok_ref[0, 0, r]
        return pltpu.make_async_copy(x_hbm.at[pl.ds(tok * SUBLANES, SUBLANES)],
                                     xg.at[pl.ds(r * SUBLANES, SUBLANES)], sem)

    def issue(r, c):
        row_copy(r).start()
        return c

    def drain(r, c):
        row_copy(r).wait()
        return c

    lax.fori_loop(0, tb, issue, 0)
    lax.fori_loop(0, tb, drain, 0)
    x = jnp.concatenate([xg[pl.ds(c, tb, stride=SUBLANES), :] for c in range(ROW_TILE_CHUNKS)], axis=1)
    hcat = _dot(x, w1_ref[0, 0]) + b1_ref[0, 0]
    glu = jnp.minimum(hcat[:, :D_FF], SWIGLU_LIMIT)
    lin = jnp.clip(hcat[:, D_FF:], -SWIGLU_LIMIT, SWIGLU_LIMIT)
    act = glu * jax.nn.sigmoid(SWIGLU_ALPHA * glu) * (lin + 1.0)
    y = _dot(act, w2_ref[0, 0]) + b2_ref[0, 0]
    for c in range(ROW_TILE_CHUNKS):
        y_ref[pl.ds(c, tb, stride=SUBLANES), :] = y[:, c * LANES:(c + 1) * LANES]


def _experts(xm_tiles, blk_e, row_tok, nblk, tb, layer, w1, b1, w2, b2):
    gs = pltpu.PrefetchScalarGridSpec(
        num_scalar_prefetch=1,
        grid=(nblk,),
        in_specs=[pl.BlockSpec((1, 1, tb), lambda i, be: (i, 0, 0), memory_space=pltpu.SMEM),
                  pl.BlockSpec(memory_space=pl.ANY),
                  pl.BlockSpec((1, 1, D_MODEL, 2 * D_FF), lambda i, be: (layer, be[i], 0, 0)),
                  pl.BlockSpec((1, 1, 1, 2 * D_FF), lambda i, be: (layer, be[i], 0, 0)),
                  pl.BlockSpec((1, 1, D_FF, D_MODEL), lambda i, be: (layer, be[i], 0, 0)),
                  pl.BlockSpec((1, 1, 1, D_MODEL), lambda i, be: (layer, be[i], 0, 0))],
        out_specs=pl.BlockSpec((tb * SUBLANES, LANES), lambda i, be: (i, 0)),
        scratch_shapes=[pltpu.VMEM((tb * SUBLANES, LANES), F32), pltpu.SemaphoreType.DMA(())],
    )
    depth, ne = b1.shape[:2]
    return pl.pallas_call(
        functools.partial(_expert_kernel, tb),
        grid_spec=gs,
        out_shape=jax.ShapeDtypeStruct((nblk * tb * SUBLANES, LANES), F32),
        compiler_params=_cparams(("arbitrary",)),
        name="moe_experts",
    )(blk_e, row_tok.reshape(nblk, 1, tb), xm_tiles, w1, b1.reshape(depth, ne, 1, 2 * D_FF), w2,
      b2.reshape(depth, ne, 1, D_MODEL))


def _combine_kernel(tc, pos_ref, y_hbm, gates_ref, h1_ref, gate2_ref, o_ref, buf, sem):
    nrow = TOP_K * tc

    def row_copy(r):
        src = pos_ref[0, 0, r]
        return pltpu.make_async_copy(y_hbm.at[pl.ds(src * SUBLANES, SUBLANES)],
                                     buf.at[pl.ds(r * SUBLANES, SUBLANES)], sem)

    def issue(r, c):
        row_copy(r).start()
        return c

    def drain(r, c):
        row_copy(r).wait()
        return c

    lax.fori_loop(0, nrow, issue, 0)
    lax.fori_loop(0, nrow, drain, 0)
    gates = gates_ref[...]
    cols = []
    for c in range(ROW_TILE_CHUNKS):
        acc = jnp.zeros((tc, LANES), F32)
        for kk in range(TOP_K):
            yk = buf[pl.ds(kk * tc * SUBLANES + c, tc, stride=SUBLANES), :]
            acc = acc + yk * gates[:, kk:kk + 1]
        cols.append(acc)
    moe = jnp.concatenate(cols, axis=1)
    o_ref[...] = h1_ref[...] + gate2_ref[0] * moe


def _combine(pos, ys, gates, h1, gate2, rows_per_group):
    n = h1.shape[0]
    tc = min(256, n)
    tpg = max(rows_per_group // tc, 1)
    pos_blk = pos.reshape(n // tc, tc, TOP_K).transpose(0, 2, 1).reshape(n // tc, 1, TOP_K * tc)
    return pl.pallas_call(
        functools.partial(_combine_kernel, tc),
        grid=(n // tc,),
        in_specs=[pl.BlockSpec((1, 1, TOP_K * tc), lambda i: (i, 0, 0), memory_space=pltpu.SMEM),
                  pl.BlockSpec(memory_space=pl.ANY),
                  pl.BlockSpec((tc, LANES), lambda i: (i, 0)),
                  pl.BlockSpec((tc, D_MODEL), lambda i: (i, 0)),
                  _mod_spec(gate2, tpg)],
        out_specs=pl.BlockSpec((tc, D_MODEL), lambda i: (i, 0)),
        out_shape=jax.ShapeDtypeStruct((n, D_MODEL), F32),
        scratch_shapes=[pltpu.VMEM((TOP_K * tc * SUBLANES, LANES), F32), pltpu.SemaphoreType.DMA(())],
        compiler_params=_cparams(("arbitrary",)),
        name="moe_combine",
    )(pos_blk, ys, gates, h1, gate2)


def _moe(h1, xm_tiles, gates, eidx, gate2, rows_per_group, layer, w1, b1, w2, b2):
    n = h1.shape[0]
    tb = 512 if n * TOP_K >= 8192 else 128
    pos, blk_e, row_tok, nblk = _moe_plan(eidx, tb)
    ys = _experts(xm_tiles, blk_e, row_tok, nblk, tb, layer, w1, b1, w2, b2)
    return _combine(pos, ys, gates, h1, gate2, rows_per_group)


PAGES_PER_STEP = 8


def _softmax_step(s, m_scr, l_scr):
    m_prev = m_scr[...]
    m_new = jnp.maximum(m_prev, jnp.max(s, axis=-1, keepdims=True))
    alpha = jnp.exp(m_prev - m_new)
    pr = jnp.exp(s - m_new[:, :1])
    l_scr[...] = alpha * l_scr[...] + jnp.sum(pr, axis=-1, keepdims=True)
    m_scr[...] = m_new
    return alpha[:, :1], pr


def _mla_decode_kernel(npg, pt_ref, q_ref, cnew_ref, krnew_ref, gk_ref, t64_ref, mask_ref, wkbt_ref, wvb_ref,
                       *rest):
    ckv_refs, kr_refs = rest[:npg], rest[npg:2 * npg]
    o_ref, m_scr, l_scr, acc_scr = rest[2 * npg:]
    del pt_ref
    c = pl.program_id(1)

    @pl.when(c == 0)
    def _():
        m_scr[...] = jnp.full(m_scr.shape, NEG, F32)
        l_scr[...] = jnp.zeros_like(l_scr)
        acc_scr[...] = jnp.zeros_like(acc_scr)

    q8 = q_ref[0]
    mask = mask_ref[...]
    qmat = _dot(q8[:, :NOPE] * gk_ref[...], t64_ref[...]) * mask
    qr = q8[:, NOPE:NOPE + ROPE_D]
    cc = jnp.concatenate([r[0] for r in ckv_refs], axis=0).astype(BF16)
    kt = _dot_nt(wkbt_ref[...], cc)
    ssq = _dot(mask, kt * kt)
    tt = _dot(qmat, kt)
    krt = jnp.concatenate([r[0] for r in kr_refs], axis=1)
    s = tt * lax.rsqrt(ssq * (1.0 / NOPE) + NORM_EPS) + _dot(qr, krt)
    alpha, pr = _softmax_step(s, m_scr, l_scr)
    acc_scr[...] = alpha * acc_scr[...] + _dot(pr, cc)

    @pl.when(c == pl.num_programs(1) - 1)
    def _():
        cnew = cnew_ref[0]
        knew = _dot_nt(jnp.broadcast_to(cnew, (SUBLANES, KV_LORA)), wkbt_ref[...])
        tt_n = jnp.sum(qmat * knew, axis=-1, keepdims=True)
        ssq_n = jnp.sum(mask * knew * knew, axis=-1, keepdims=True)
        s_n = (tt_n * lax.rsqrt(ssq_n * (1.0 / NOPE) + NORM_EPS)
               + jnp.sum(qr * krnew_ref[0][:, :ROPE_D], axis=-1, keepdims=True))
        alpha_n, p_n = _softmax_step(s_n, m_scr, l_scr)
        acc = alpha_n * acc_scr[...] + p_n * cnew
        lat = acc / l_scr[...][:, :1]
        o8 = _dot(lat, wvb_ref[...]) * mask
        o_ref[0] = jnp.sum(o8, axis=0, keepdims=True)


def _mla_decode(page_table, q_s, ckv_s, kr_s, ckv_cache, krt_cache, w_kvb, k_norm):
    bd, n_pages = page_table.shape
    npg = PAGES_PER_STEP
    wkv = w_kvb.reshape(KV_LORA, H_A, NOPE + V_DIM)
    wkbt = wkv[:, :, :NOPE].reshape(KV_LORA, H_A * NOPE).T.astype(BF16)
    wvb = wkv[:, :, NOPE:].reshape(KV_LORA, H_A * V_DIM).astype(BF16)
    hh = np.arange(H_A)[:, None]
    mask = jnp.asarray((np.arange(H_A * NOPE)[None, :] // NOPE) == hh, F32)
    t64 = jnp.asarray(np.tile(np.eye(NOPE, dtype=np.float32), (1, H_A)), BF16)
    gk = k_norm[:NOPE][None]
    q8 = q_s.reshape(bd, H_A, LANES)
    consts = [gk, t64, mask, wkbt, wvb]

    def page_spec(shape, k):
        return pl.BlockSpec((1,) + shape, lambda b, c, pt: (pt[b, c * npg + k], 0, 0))

    gs = pltpu.PrefetchScalarGridSpec(
        num_scalar_prefetch=1,
        grid=(bd, n_pages // npg),
        in_specs=([pl.BlockSpec((1, H_A, LANES), lambda b, c, pt: (b, 0, 0)),
                   pl.BlockSpec((1, 1, KV_LORA), lambda b, c, pt: (b, 0, 0)),
                   pl.BlockSpec((1, 1, LANES), lambda b, c, pt: (b, 0, 0))]
                  + [pl.BlockSpec(a.shape, lambda b, c, pt: (0, 0)) for a in consts]
                  + [page_spec((PAGE, KV_LORA), k) for k in range(npg)]
                  + [page_spec((ROPE_D, PAGE), k) for k in range(npg)]),
        out_specs=pl.BlockSpec((1, 1, H_A * V_DIM), lambda b, c, pt: (b, 0, 0)),
        scratch_shapes=[pltpu.VMEM((SUBLANES, LANES), F32), pltpu.VMEM((SUBLANES, LANES), F32),
                        pltpu.VMEM((SUBLANES, KV_LORA), F32)],
    )
    out = pl.pallas_call(
        functools.partial(_mla_decode_kernel, npg),
        grid_spec=gs,
        out_shape=jax.ShapeDtypeStruct((bd, 1, H_A * V_DIM), F32),
        compiler_params=_cparams(("arbitrary", "arbitrary")),
        name="l0_mla_decode",
    )(page_table, q8, ckv_s.reshape(bd, 1, KV_LORA), kr_s.reshape(bd, 1, LANES), *consts,
      *([ckv_cache] * npg), *([krt_cache] * npg))
    return out.reshape(bd, H_A * V_DIM)


def _fox_decode_kernel(npg, pt_ref, q_ref, knew_ref, vnew_ref, lfnew_ref, t64_ref, mask_ref, utri_ref,
                       pe_ref, po_ref, *rest):
    kv_refs, lf_refs = rest[:npg], rest[npg:2 * npg]
    o_ref, m_scr, l_scr, acc_scr, carry = rest[2 * npg:]
    del pt_ref
    c = pl.program_id(1)
    nkv = KV_B * HEAD_DIM

    @pl.when(c == 0)
    def _():
        m_scr[...] = jnp.full(m_scr.shape, NEG, F32)
        l_scr[...] = jnp.zeros_like(l_scr)
        acc_scr[...] = jnp.zeros_like(acc_scr)
        carry[...] = jnp.zeros_like(carry)

    mask = mask_ref[...]
    qblk = _dot(q_ref[0], t64_ref[...]) * mask
    kt = jnp.concatenate([r[0, :nkv, :] for r in kv_refs], axis=1)
    vt = jnp.concatenate([r[0, nkv:, :] for r in kv_refs], axis=1)
    lft = jnp.concatenate([r[0] for r in lf_refs], axis=1)
    hi, mid, lo = _split3(lft)
    cf3 = jnp.dot(jnp.concatenate([hi, mid, lo], axis=0), utri_ref[...], preferred_element_type=F32)
    cf = cf3[:H_B] + cf3[H_B:2 * H_B] + cf3[2 * H_B:] + carry[...][:, :1]
    carry[...] = jnp.broadcast_to(cf[:, cf.shape[1] - 1:], carry.shape)
    s = _dot(qblk, kt) - cf
    alpha, pr = _softmax_step(s, m_scr, l_scr)
    acc_scr[...] = alpha * acc_scr[...] + _dot_nt(pr, vt)

    @pl.when(c == pl.num_programs(1) - 1)
    def _():
        cf_t = carry[...][:, :1] + lfnew_ref[0][:, :1]
        s_n = jnp.sum(qblk * knew_ref[0], axis=-1, keepdims=True) - cf_t
        alpha_n, p_n = _softmax_step(s_n, m_scr, l_scr)
        acc = (alpha_n * acc_scr[...] + p_n * vnew_ref[0]) / l_scr[...][:, :1] * mask
        row = lax.broadcasted_iota(I32, acc.shape, 0)
        even = jnp.where(row % 2 == 0, acc, 0.0)
        odd = jnp.where(row % 2 == 1, acc, 0.0)
        o8 = _dot(even, pe_ref[...]) + _dot(odd, po_ref[...])
        o_ref[0] = jnp.sum(o8, axis=0, keepdims=True)


def _fox_decode(page_table, fq_s, fkv_s, lf_s, kvt_cache, lft_cache):
    bd, n_pages = page_table.shape
    npg = PAGES_PER_STEP
    nkv = KV_B * HEAD_DIM
    pc = npg * PAGE
    hh = np.arange(H_B)[:, None]
    mask = jnp.asarray((np.arange(nkv)[None, :] // HEAD_DIM) == hh // 2, F32)
    t64 = jnp.asarray(np.tile(np.eye(HEAD_DIM, dtype=np.float32), (1, KV_B)), BF16)
    utri = jnp.asarray(np.triu(np.ones((pc, pc), np.float32)), BF16)
    pe = np.zeros((nkv, H_B * HEAD_DIM), np.float32)
    po = np.zeros((nkv, H_B * HEAD_DIM), np.float32)
    dd = np.arange(HEAD_DIM)
    for j in range(KV_B):
        pe[j * HEAD_DIM + dd, (2 * j) * HEAD_DIM + dd] = 1.0
        po[j * HEAD_DIM + dd, (2 * j + 1) * HEAD_DIM + dd] = 1.0
    consts = [t64, mask, utri, jnp.asarray(pe, BF16), jnp.asarray(po, BF16)]
    lfnew = jnp.broadcast_to(lf_s[:, :H_B, None], (bd, H_B, LANES))

    def page_spec(shape, k):
        return pl.BlockSpec((1,) + shape, lambda b, c, pt: (pt[b, c * npg + k], 0, 0))

    gs = pltpu.PrefetchScalarGridSpec(
        num_scalar_prefetch=1,
        grid=(bd, n_pages // npg),
        in_specs=([pl.BlockSpec((1, H_B, HEAD_DIM), lambda b, c, pt: (b, 0, 0)),
                   pl.BlockSpec((1, 1, nkv), lambda b, c, pt: (b, 0, 0)),
                   pl.BlockSpec((1, 1, nkv), lambda b, c, pt: (b, 0, 0)),
                   pl.BlockSpec((1, H_B, LANES), lambda b, c, pt: (b, 0, 0))]
                  + [pl.BlockSpec(a.shape, lambda b, c, pt: (0, 0)) for a in consts]
                  + [page_spec((2 * nkv, PAGE), k) for k in range(npg)]
                  + [page_spec((H_B, PAGE), k) for k in range(npg)]),
        out_specs=pl.BlockSpec((1, 1, H_B * HEAD_DIM), lambda b, c, pt: (b, 0, 0)),
        scratch_shapes=[pltpu.VMEM((SUBLANES, LANES), F32), pltpu.VMEM((SUBLANES, LANES), F32),
                        pltpu.VMEM((SUBLANES, nkv), F32), pltpu.VMEM((SUBLANES, LANES), F32)],
    )
    out = pl.pallas_call(
        functools.partial(_fox_decode_kernel, npg),
        grid_spec=gs,
        out_shape=jax.ShapeDtypeStruct((bd, 1, H_B * HEAD_DIM), F32),
        compiler_params=_cparams(("arbitrary", "arbitrary")),
        name="l0_fox_decode",
    )(page_table, fq_s.reshape(bd, H_B, HEAD_DIM), fkv_s[:, :nkv].reshape(bd, 1, nkv),
      fkv_s[:, nkv:].reshape(bd, 1, nkv), lfnew, *consts, *([kvt_cache] * npg), *([lft_cache] * npg))
    return out.reshape(bd, H_B * HEAD_DIM)


N_GROUPS = len(C_WINDOWS)
GROUP_COLS = 3 * H_C * HEAD_DIM
HC_COLS = H_C * HEAD_DIM


def _rot_half64(x):
    n = x.shape[1]
    lane = _lane_iota(x.shape)
    fwd = pltpu.roll(x, n - HEAD_DIM // 2, axis=1)
    bwd = pltpu.roll(x, HEAD_DIM // 2, axis=1)
    return jnp.where(lane % HEAD_DIM < HEAD_DIM // 2, -fwd, bwd)


def _l1_qkv(u, g, s_f, gq, gk, cos, sin):
    base = g * GROUP_COLS
    q = u[:, base:base + HC_COLS]
    k = u[:, base + HC_COLS:base + 2 * HC_COLS]
    v = u[:, base + 2 * HC_COLS:base + 3 * HC_COLS]
    qn = q * _group_rms(q, s_f, 1.0 / HEAD_DIM) * gq
    kn = k * _group_rms(k, s_f, 1.0 / HEAD_DIM) * gk
    qn = qn * cos + _rot_half64(qn) * sin
    kn = kn * cos + _rot_half64(kn) * sin
    return qn, kn, v


def _l1_tables(pos):
    half = HEAD_DIM // 2
    inv_freq = ROPE_THETA ** (-jnp.arange(half, dtype=F32) / half)
    ang = pos.astype(F32)[:, None] * inv_freq[None, :]
    return jnp.concatenate([jnp.cos(ang)] * 4, axis=1), jnp.concatenate([jnp.sin(ang)] * 4, axis=1)


def _l1_prompt_kernel(tm, x_ref, shift_ref, scale_ref, g_ref, w_ref, sf_ref, gq_ref, gk_ref, cos_ref, sin_ref,
                      *rest):
    outs, kvlast_ref, scr = rest[:3 * N_GROUPS], rest[3 * N_GROUPS], rest[3 * N_GROUPS + 1]
    xm = _modulate(x_ref[...], g_ref[...], scale_ref[0], shift_ref[0])
    u = _dot(xm, w_ref[...])
    cos = _tile_lanes(cos_ref[...], HC_COLS // LANES)
    sin = _tile_lanes(sin_ref[...], HC_COLS // LANES)
    for g in range(N_GROUPS):
        d = C_DILATIONS[g]
        qn, kn, v = _l1_qkv(u, g, sf_ref[...], gq_ref[...], gk_ref[...], cos, sin)
        kvlast_ref[0, :, g * 2 * HC_COLS:g * 2 * HC_COLS + HC_COLS] = kn
        kvlast_ref[0, :, g * 2 * HC_COLS + HC_COLS:(g + 1) * 2 * HC_COLS] = v
        for t, val in enumerate((qn, kn, v)):
            o_ref = outs[3 * g + t]
            if d == 1:
                o_ref[0, 0] = val.astype(BF16)
            else:
                for cc in range(HC_COLS // LANES):
                    scr[cc] = val[:, cc * LANES:(cc + 1) * LANES]
                for r in range(d):
                    o_ref[0, r] = jnp.concatenate(
                        [scr[cc, pl.ds(r, tm // d, stride=d), :] for cc in range(HC_COLS // LANES)],
                        axis=1).astype(BF16)


def _l1_prompt_proj(x, shift, scale, g, w_in, q_norm, k_norm, batch, seq):
    n, dm = x.shape
    tm = 256
    tps = seq // tm
    wmax = max(C_WINDOWS)
    assert seq >= wmax and wmax % tm == 0
    w = w_in.astype(BF16)
    li = np.arange(LANES)
    sf = jnp.asarray((li[:, None] // HEAD_DIM) == (li[None, :] // HEAD_DIM), BF16)
    gq = (jnp.tile(q_norm, H_C) * HD_SCALE)[None]
    gk = jnp.tile(k_norm, H_C)[None]
    cos, sin = _l1_tables(jnp.arange(seq, dtype=I32))
    tab = pl.BlockSpec((tm, LANES), lambda i: (i % tps, 0))
    first_kept = tps - wmax // tm
    out_specs, out_shapes = [], []
    for gi in range(N_GROUPS):
        d = C_DILATIONS[gi]
        for _ in range(3):
            out_specs.append(pl.BlockSpec((1, d, tm // d, HC_COLS), lambda i: (i // tps, 0, i % tps, 0)))
            out_shapes.append(jax.ShapeDtypeStruct((batch, d, seq // d, HC_COLS), BF16))
    out_specs.append(pl.BlockSpec((1, tm, 2 * N_GROUPS * HC_COLS),
                                  lambda i: (i // tps, jnp.maximum(i % tps - first_kept, 0), 0)))
    out_shapes.append(jax.ShapeDtypeStruct((batch, wmax, 2 * N_GROUPS * HC_COLS), F32))
    ins = [x, shift, scale, g, w, sf, gq, gk, cos, sin]
    return pl.pallas_call(
        functools.partial(_l1_prompt_kernel, tm),
        grid=(n // tm,),
        in_specs=[pl.BlockSpec((tm, dm), lambda i: (i, 0)), _mod_spec(shift, tps), _mod_spec(scale, tps),
                  _const_spec(g), _const_spec(w), _const_spec(sf), _const_spec(gq), _const_spec(gk), tab, tab],
        out_specs=out_specs,
        out_shape=out_shapes,
        scratch_shapes=[pltpu.VMEM((HC_COLS // LANES, tm, LANES), F32)],
        compiler_params=_cparams(("arbitrary",)),
        name="l1_prompt_proj",
    )(*ins)


def _dsa_kernel(bpc, q_ref, kc_ref, kp_ref, vc_ref, vp_ref, num_ref, m_ref, den_ref):
    n = pl.program_id(1)
    tq = q_ref.shape[1]
    lo = jnp.where(n % bpc == 0, tq, 0)
    q = q_ref[0]
    kcat = jnp.concatenate([kp_ref[0], kc_ref[0]], axis=0)
    vcat = jnp.concatenate([vp_ref[0], vc_ref[0]], axis=0)
    a = lax.broadcasted_iota(I32, (tq, 2 * tq), 0)
    c = lax.broadcasted_iota(I32, (tq, 2 * tq), 1)
    ok = (c >= a) & (c <= a + tq) & (c >= lo)
    lane_kv = _lane_iota((2 * tq, LANES))
    lane_o = _lane_iota((tq, LANES))
    m_all = jnp.zeros((tq, LANES), F32)
    den_all = jnp.ones((tq, LANES), F32)
    for p in range(H_C // 2):
        qp = q[:, p * LANES:(p + 1) * LANES]
        kp_ = kcat[:, p * LANES:(p + 1) * LANES]
        vp_ = vcat[:, p * LANES:(p + 1) * LANES]
        num_pair = jnp.zeros((tq, LANES), F32)
        for hh in range(2):
            hm = (lane_kv // HEAD_DIM) == hh
            s = _dot_nt(qp, jnp.where(hm, kp_, jnp.zeros_like(kp_)))
            s = jnp.where(ok, s, NEG)
            m = jnp.max(s, axis=-1, keepdims=True)
            e = jnp.exp(s - m)
            den = jnp.sum(e, axis=-1, keepdims=True)
            num_pair = num_pair + _dot(e, jnp.where(hm, vp_, jnp.zeros_like(vp_)))
            m_all = jnp.where(lane_o == 2 * p + hh, m, m_all)
            den_all = jnp.where(lane_o == 2 * p + hh, den, den_all)
        num_ref[0, :, p * LANES:(p + 1) * LANES] = num_pair
    m_ref[0] = m_all
    den_ref[0] = den_all


def _dsa_attention(q, k, v, dil):
    b, s, _ = q.shape
    tq = PAGE
    bpc = (s // dil) // tq
    cur = pl.BlockSpec((1, tq, HC_COLS), lambda bb, n: (bb, n, 0))
    prev = pl.BlockSpec((1, tq, HC_COLS), lambda bb, n: (bb, jnp.maximum(n - 1, 0), 0))
    stat = pl.BlockSpec((1, tq, LANES), lambda bb, n: (bb, n, 0))
    return pl.pallas_call(
        functools.partial(_dsa_kernel, bpc),
        grid=(b, s // tq),
        in_specs=[cur, cur, prev, cur, prev],
        out_specs=[cur, stat, stat],
        out_shape=[jax.ShapeDtypeStruct((b, s, HC_COLS), F32), jax.ShapeDtypeStruct((b, s, LANES), F32),
                   jax.ShapeDtypeStruct((b, s, LANES), F32)],
        compiler_params=_cparams(("arbitrary", "arbitrary")),
        name="l1_dsa_attention",
    )(q, k, k, v, v)


def _l1_out_kernel(tm, res_ref, gate1_ref, g2_ref, shift2_ref, scale2_ref, wo_ref, wr_ref, br_ref, eh_ref,
                   *rest):
    parts = rest[:3 * N_GROUPS]
    h1_ref, xm_ref, gate_ref, eidx_ref = rest[3 * N_GROUPS:3 * N_GROUPS + 4]
    scr = rest[3 * N_GROUPS + 4:]
    vals = []
    for g in range(N_GROUPS):
        d = C_DILATIONS[g]
        for t in range(3):
            ref, sc = parts[3 * g + t], scr[3 * g + t]
            if d == 1:
                vals.append(ref[0, 0])
            else:
                nch = sc.shape[0]
                for r in range(d):
                    blk = ref[0, r]
                    for cc in range(nch):
                        sc[cc, pl.ds(r, tm // d, stride=d), :] = blk[:, cc * LANES:(cc + 1) * LANES]
                vals.append(sc[0] if nch == 1 else jnp.concatenate([sc[cc] for cc in range(nch)], axis=1))
    nums, ms, dens = vals[0::3], vals[1::3], vals[2::3]
    mx = jnp.maximum(jnp.maximum(ms[0], ms[1]), ms[2])
    ws = [jnp.exp(mm - mx) for mm in ms]
    dsum = ws[0] * dens[0] + ws[1] * dens[1] + ws[2] * dens[2]
    o = jnp.zeros(nums[0].shape, F32)
    for g in range(N_GROUPS):
        o = o + _dot3(ws[g] / dsum, eh_ref[...]) * nums[g]
    h1 = res_ref[...] + gate1_ref[0] * _dot(o, wo_ref[...])
    h1_ref[...] = h1
    _router_epilogue(h1, g2_ref[...], shift2_ref[0], scale2_ref[0], wr_ref, br_ref, xm_ref, gate_ref, eidx_ref)


def _head_expand_mat():
    eh = np.zeros((LANES, HC_COLS), np.float32)
    for h in range(H_C):
        eh[h, h * HEAD_DIM:(h + 1) * HEAD_DIM] = 1.0
    return jnp.asarray(eh, BF16)


def _l1_out(parts, w_out, res, gate1, g2, shift2, scale2, wr, br, batch, seq):
    n = res.shape[0]
    tm = 256
    tps = seq // tm
    wo = w_out.astype(BF16)
    eh = _head_expand_mat()
    ins = [res, gate1, g2, shift2, scale2, wo, wr, br, eh]
    in_specs = [pl.BlockSpec((tm, D_MODEL), lambda i: (i, 0)), _mod_spec(gate1, tps), _const_spec(g2),
                _mod_spec(shift2, tps), _mod_spec(scale2, tps), _const_spec(wo), _const_spec(wr),
                _const_spec(br), _const_spec(eh)]
    scratch = []
    for gi in range(N_GROUPS):
        d = C_DILATIONS[gi]
        for t, arr in enumerate(parts[gi]):
            w = arr.shape[-1]
            ins.append(arr.reshape(batch, d, seq // d, w))
            in_specs.append(pl.BlockSpec((1, d, tm // d, w), lambda i: (i // tps, 0, i % tps, 0)))
            scratch.append(pltpu.VMEM((w // LANES, tm, LANES), F32))
    out_specs, out_shapes = _router_out_specs(n, tm)
    return pl.pallas_call(
        functools.partial(_l1_out_kernel, tm),
        grid=(n // tm,),
        in_specs=in_specs,
        out_specs=out_specs,
        out_shape=out_shapes,
        scratch_shapes=scratch,
        compiler_params=_cparams(("arbitrary",)),
        name="l1_out_router",
    )(*ins)


def _l1_sample_kernel(x_ref, shift_ref, scale_ref, g_ref, w_ref, sf_ref, gq_ref, gk_ref, cos_ref, sin_ref,
                      q_ref, k_ref, v_ref):
    xm = _modulate(x_ref[...], g_ref[...], scale_ref[0], shift_ref[0])
    u = _dot(xm, w_ref[...])
    cos = _tile_lanes(cos_ref[...], HC_COLS // LANES)
    sin = _tile_lanes(sin_ref[...], HC_COLS // LANES)
    for g in range(N_GROUPS):
        qn, kn, v = _l1_qkv(u, g, sf_ref[...], gq_ref[...], gk_ref[...], cos, sin)
        q_ref[:, g * HC_COLS:(g + 1) * HC_COLS] = qn
        k_ref[:, g * HC_COLS:(g + 1) * HC_COLS] = kn
        v_ref[:, g * HC_COLS:(g + 1) * HC_COLS] = v


def _l1_sample_proj(x, shift, scale, g, w_in, q_norm, k_norm, past):
    n = x.shape[0]
    w = w_in.astype(BF16)
    li = np.arange(LANES)
    sf = jnp.asarray((li[:, None] // HEAD_DIM) == (li[None, :] // HEAD_DIM), BF16)
    gq = (jnp.tile(q_norm, H_C) * HD_SCALE)[None]
    gk = jnp.tile(k_norm, H_C)[None]
    cos, sin = _l1_tables(jnp.full((1,), past, I32))
    ins = [x, shift, scale, g, w, sf, gq, gk, cos, sin]
    wd = N_GROUPS * HC_COLS
    return pl.pallas_call(
        _l1_sample_kernel,
        grid=(1,),
        in_specs=[_const_spec(a) for a in ins],
        out_specs=[pl.BlockSpec((n, wd), lambda i: (0, 0))] * 3,
        out_shape=[jax.ShapeDtypeStruct((n, wd), F32)] * 3,
        compiler_params=_cparams(("arbitrary",)),
        name="l1_sample_proj",
    )(*ins)


DECODE_HEADS_PER_STEP = 4


def _dsa_decode_kernel(q_ref, kcol_ref, vcol_ref, krow_ref, vrow_ref, *rest):
    st = rest[:N_GROUPS]
    o_ref = rest[N_GROUPS]
    new = rest[N_GROUPS + 1:]
    for hh in range(DECODE_HEADS_PER_STEP):
        nums, ms, dens = [], [], []
        for g in range(N_GROUPS):
            w, d = C_WINDOWS[g], C_DILATIONS[g]
            kt = st[g][0, 0, hh]
            vt = st[g][0, 1, hh]
            q = q_ref[0, g, hh]
            s = _dot(jnp.broadcast_to(q, (SUBLANES, HEAD_DIM)), kt)[:1]
            lane = _lane_iota(s.shape)
            s = jnp.where(lane % d == 0, s, NEG)
            s_new = jnp.sum(q * krow_ref[0, g, hh], axis=-1, keepdims=True)
            m = jnp.maximum(jnp.max(s, axis=-1, keepdims=True), s_new)
            e = jnp.exp(s - m)
            e_new = jnp.exp(s_new - m)
            dens.append(jnp.sum(e, axis=-1, keepdims=True) + e_new)
            nums.append(_dot_nt(jnp.broadcast_to(e, (SUBLANES, w)), vt)[:1] + e_new * vrow_ref[0, g, hh])
            ms.append(m)
            lane2 = _lane_iota(kt.shape)
            new[g][0, 0, hh] = jnp.where(lane2 == w - 1, kcol_ref[0, g, hh], pltpu.roll(kt, w - 1, axis=1))
            new[g][0, 1, hh] = jnp.where(lane2 == w - 1, vcol_ref[0, g, hh], pltpu.roll(vt, w - 1, axis=1))
        mx = jnp.maximum(jnp.maximum(ms[0], ms[1]), ms[2])
        ws = [jnp.exp(mm - mx) for mm in ms]
        num = ws[0] * nums[0] + ws[1] * nums[1] + ws[2] * nums[2]
        den = ws[0] * dens[0] + ws[1] * dens[1] + ws[2] * dens[2]
        o_ref[0, hh] = num / den


def _dsa_decode(q, k, v, states):
    bd = q.shape[0]
    hps = DECODE_HEADS_PER_STEP
    q5 = q.reshape(bd, N_GROUPS, H_C, 1, HEAD_DIM)
    krow = k.reshape(bd, N_GROUPS, H_C, 1, HEAD_DIM)
    vrow = v.reshape(bd, N_GROUPS, H_C, 1, HEAD_DIM)
    kcol = k.reshape(bd, N_GROUPS, H_C, HEAD_DIM, 1)
    vcol = v.reshape(bd, N_GROUPS, H_C, HEAD_DIM, 1)
    row_spec = pl.BlockSpec((1, N_GROUPS, hps, 1, HEAD_DIM), lambda b, h: (b, 0, h, 0, 0))
    col_spec = pl.BlockSpec((1, N_GROUPS, hps, HEAD_DIM, 1), lambda b, h: (b, 0, h, 0, 0))
    st_specs = [pl.BlockSpec((1, 2, hps, HEAD_DIM, w), lambda b, h: (b, 0, h, 0, 0)) for w in C_WINDOWS]
    outs = pl.pallas_call(
        _dsa_decode_kernel,
        grid=(bd, H_C // hps),
        in_specs=[row_spec, col_spec, col_spec, row_spec, row_spec] + st_specs,
        out_specs=[pl.BlockSpec((1, hps, 1, HEAD_DIM), lambda b, h: (b, h, 0, 0))] + st_specs,
        out_shape=[jax.ShapeDtypeStruct((bd, H_C, 1, HEAD_DIM), F32)]
        + [jax.ShapeDtypeStruct(s.shape, F32) for s in states],
        compiler_params=_cparams(("arbitrary", "arbitrary")),
        name="l1_dsa_decode",
    )(q5, kcol, vcol, krow, vrow, *states)
    return outs[0].reshape(bd, HC_COLS), outs[1:]


def _out_router_plain(a, w_out, res, gate1, g2, shift2, scale2, wr, br, rows_per_group):
    return _out_router(a, w_out, res, gate1, g2, shift2, scale2, wr, br, rows_per_group)


def kernel(x_prompt, x_sample, cache_mla_ckv, cache_mla_krope, cache_fox_kv, cache_fox_logf, state_c1_kv, state_c2_kv, state_c3_kv, page_table, c_prompt, c_sample, ada_w, ada_b, norm1_g, norm2_g, ab_w_in, mla_qa_norm, mla_kv_norm, mla_w_qb, mla_w_kvb, mla_q_norm, mla_k_norm, fox_q_norm, fox_k_norm, fox_f_bias, ab_w_out, c_w_in, c_q_norm, c_k_norm, c_w_out, router_w, router_b, moe_w1, moe_b1, moe_w2, moe_b2):
    b, s, d = x_prompt.shape
    bd = x_sample.shape[0]
    assert x_sample.shape[1] == 1 and ada_w.shape[0] == 2
    n_pages = page_table.shape[1]
    past = n_pages * PAGE
    assert past >= max(C_WINDOWS) and n_pages % PAGES_PER_STEP == 0
    n = b * s

    pad = (-(b + bd)) % SUBLANES
    c_all = jnp.concatenate([c_prompt, c_sample, jnp.zeros((pad, d), F32)], axis=0)
    mod = _adaln(c_all, ada_w, ada_b)

    def mods(layer):
        mp = [mod[layer, :b, i * d:(i + 1) * d][:, None, :] for i in range(6)]
        ms = [mod[layer, b:b + bd, i * d:(i + 1) * d][None] for i in range(6)]
        return mp, ms

    hp = x_prompt.reshape(n, d)
    hs = x_sample.reshape(bd, d)

    mp, ms = mods(0)
    g1, g2 = norm1_g[0][None], norm2_g[0][None]
    wr, br = _router_consts(router_w[0], router_b[0])
    consts = _l0_consts(ab_w_in[0], mla_qa_norm[0], mla_kv_norm[0], mla_w_qb[0], mla_w_kvb[0], mla_q_norm[0],
                        mla_k_norm[0], fox_q_norm[0], fox_k_norm[0], fox_f_bias[0])
    q, k, v, p_ckv, p_kr, p_fkv, p_lf = _l0_prompt_proj(hp, mp[0], mp[1], g1, consts, s)
    o = _flash_attention(q.reshape(b, s, -1), k.reshape(b, s, -1), v.reshape(b, s, -1))
    h1, xm, gates, eidx = _out_router(o.reshape(n, -1), ab_w_out[0], hp, mp[2], g2, mp[3], mp[4], wr, br, s)
    hp = _moe(h1, xm, gates, eidx, mp[5], s, 0, moe_w1, moe_b1, moe_w2, moe_b2)

    q_s, s_ckv, s_kr, fq_s, s_fkv, s_lf = _l0_sample_proj(hs, ms[0], ms[1], g1, consts, past)
    npool = cache_mla_ckv.shape[1]
    krt_cache = jnp.transpose(cache_mla_krope[0], (0, 2, 1))
    kvt_cache = jnp.transpose(cache_fox_kv[0], (0, 2, 3, 4, 1)).reshape(npool, 2 * KV_B * HEAD_DIM, PAGE)
    lft_cache = jnp.transpose(cache_fox_logf[0], (0, 2, 1))
    oa = _mla_decode(page_table, q_s, s_ckv, s_kr, cache_mla_ckv[0], krt_cache, mla_w_kvb[0], mla_k_norm[0])
    ob = _fox_decode(page_table, fq_s, s_fkv, s_lf, kvt_cache, lft_cache)
    o_s = jnp.concatenate([oa, ob], axis=1).astype(BF16)
    h1, xm, gates, eidx = _out_router(o_s, ab_w_out[0], hs, ms[2], g2, ms[3], ms[4], wr, br, bd)
    hs = _moe(h1, xm, gates, eidx, ms[5], bd, 0, moe_w1, moe_b1, moe_w2, moe_b2)

    mp, ms = mods(1)
    g1, g2 = norm1_g[1][None], norm2_g[1][None]
    wr, br = _router_consts(router_w[1], router_b[1])
    outs = _l1_prompt_proj(hp, mp[0], mp[1], g1, c_w_in[0], c_q_norm[0], c_k_norm[0], b, s)
    kvlast = outs[3 * N_GROUPS]
    parts = []
    for gi in range(N_GROUPS):
        qg, kg, vg = (a.reshape(b, s, HC_COLS) for a in outs[3 * gi:3 * gi + 3])
        parts.append(_dsa_attention(qg, kg, vg, C_DILATIONS[gi]))
    h1, xm, gates, eidx = _l1_out(parts, c_w_out[0], hp, mp[2], g2, mp[3], mp[4], wr, br, b, s)
    hp = _moe(h1, xm, gates, eidx, mp[5], s, 1, moe_w1, moe_b1, moe_w2, moe_b2)

    q1, k1, v1 = _l1_sample_proj(hs, ms[0], ms[1], g1, c_w_in[0], c_q_norm[0], c_k_norm[0], past)
    states = [jnp.transpose(st[0], (0, 2, 3, 4, 1)) for st in (state_c1_kv, state_c2_kv, state_c3_kv)]
    o1, new_states = _dsa_decode(q1, k1, v1, states)
    h1, xm, gates, eidx = _out_router(o1.astype(BF16), c_w_out[0], hs, ms[2], g2, ms[3], ms[4], wr, br, bd)
    hs = _moe(h1, xm, gates, eidx, ms[5], bd, 1, moe_w1, moe_b1, moe_w2, moe_b2)

    wmax = max(C_WINDOWS)
    p_c = []
    for gi, w in enumerate(C_WINDOWS):
        blk = kvlast[:, wmax - w:, gi * 2 * HC_COLS:(gi + 1) * 2 * HC_COLS]
        p_c.append(blk.reshape(1, b, w, 2, H_C, HEAD_DIM))
    s_c = [jnp.transpose(ns, (0, 4, 1, 2, 3))[None] for ns in new_states]
    return (hp.reshape(b, s, d), hs.reshape(bd, 1, d),
            p_ckv.reshape(1, b, s, KV_LORA), p_kr.reshape(1, b, s, ROPE_D),
            p_fkv.reshape(1, b, s, 2, KV_B, HEAD_DIM), p_lf.reshape(1, b, s, H_B),
            p_c[0], p_c[1], p_c[2],
            s_ckv.reshape(1, bd, 1, KV_LORA), s_kr[:, :ROPE_D].reshape(1, bd, 1, ROPE_D),
            s_fkv.reshape(1, bd, 1, 2, KV_B, HEAD_DIM), s_lf[:, :H_B].reshape(1, bd, 1, H_B),
            s_c[0], s_c[1], s_c[2])
```

```python
import functools

import numpy as np
import jax
import jax.numpy as jnp
from jax import lax
from jax.experimental import pallas as pl
from jax.experimental.pallas import tpu as pltpu

F32 = jnp.float32
BF16 = jnp.bfloat16
I32 = jnp.int32

D_MODEL = 1024
PAGE = 128
HEAD_DIM = 64
H_A = 8
Q_LORA = 384
KV_LORA = 256
NOPE = 64
ROPE_D = 32
V_DIM = 64
H_B = 8
KV_B = 4
H_C = 8
C_WINDOWS = (128, 512, 2048)
C_DILATIONS = (1, 4, 16)
N_EXPERTS = 32
TOP_K = 4
D_FF = 1024
SWIGLU_ALPHA = 1.702
SWIGLU_LIMIT = 7.0
ROPE_THETA = 10000.0
NORM_EPS = 1e-6
NEG = -1e30
MLA_SCALE = (NOPE + ROPE_D) ** -0.5
HD_SCALE = HEAD_DIM ** -0.5

LANES = 128
SUBLANES = 8
VMEM_LIMIT = 56 * 1024 * 1024
ROW_TILE_CHUNKS = D_MODEL // LANES
FLASH_TILE = 1024
MOE_ROWS = 512
DMA_UNROLL = 8


def _cparams(sem):
    return pltpu.CompilerParams(dimension_semantics=sem, vmem_limit_bytes=VMEM_LIMIT)


def _dot(a, b):
    return jnp.dot(a.astype(BF16), b.astype(BF16), preferred_element_type=F32)


def _dot_nt(a, b):
    return lax.dot_general(a.astype(BF16), b.astype(BF16), (((1,), (1,)), ((), ())),
                           preferred_element_type=F32)


def _split3(x):
    hi = x.astype(BF16)
    r = x - hi.astype(F32)
    mid = r.astype(BF16)
    lo = (r - mid.astype(F32)).astype(BF16)
    return hi, mid, lo


def _dot3(a, b):
    hi, mid, lo = _split3(a)
    return (jnp.dot(hi, b, preferred_element_type=F32) + jnp.dot(mid, b, preferred_element_type=F32)
            + jnp.dot(lo, b, preferred_element_type=F32))


def _modulate(x, g, scale, shift):
    ms = jnp.mean(x * x, axis=-1, keepdims=True)
    return (x * lax.rsqrt(ms + NORM_EPS) * g) * (1.0 + scale) + shift


def _group_rms(x, s_blk, inv_cnt):
    sq = (x * x).astype(BF16)
    parts = [jnp.dot(sq[:, c * LANES:(c + 1) * LANES], s_blk, preferred_element_type=F32)
             for c in range(x.shape[1] // LANES)]
    ssq = parts[0] if len(parts) == 1 else jnp.concatenate(parts, axis=1)
    return lax.rsqrt(ssq * inv_cnt + NORM_EPS)


def _tile_lanes(x, n):
    return jnp.concatenate([x] * n, axis=1)


def _lane_iota(shape):
    return lax.broadcasted_iota(I32, shape, len(shape) - 1)


def _adaln_kernel(c_ref, w_ref, b_ref, o_ref):
    c = c_ref[...]
    s = c * jax.nn.sigmoid(c)
    o_ref[0] = _dot(s, w_ref[0]) + b_ref[0]


def _adaln(c_all, ada_w, ada_b):
    depth, d, n6 = ada_w.shape
    r = c_all.shape[0]
    tn = 768
    return pl.pallas_call(
        _adaln_kernel,
        grid=(depth, n6 // tn),
        in_specs=[pl.BlockSpec((r, d), lambda l, j: (0, 0)),
                  pl.BlockSpec((1, d, tn), lambda l, j: (l, 0, j)),
                  pl.BlockSpec((1, 1, tn), lambda l, j: (l, 0, j))],
        out_specs=pl.BlockSpec((1, r, tn), lambda l, j: (l, 0, j)),
        out_shape=jax.ShapeDtypeStruct((depth, r, n6), F32),
        compiler_params=_cparams(("arbitrary", "arbitrary")),
        name="adaln",
    )(c_all, ada_w, ada_b.reshape(depth, 1, n6))


def _mod_spec(arr, tiles_per_group):
    g, r, d = arr.shape
    if g == 1:
        return pl.BlockSpec((1, r, d), lambda i: (0, 0, 0))
    return pl.BlockSpec((1, r, d), lambda i: (i // tiles_per_group, 0, 0))


def _const_spec(arr):
    nd = arr.ndim
    return pl.BlockSpec(arr.shape, lambda *_: (0,) * nd)


def _cos_sin(pos, half):
    inv_freq = (np.float32(ROPE_THETA) ** (-np.arange(half, dtype=np.float32) / np.float32(half))).astype(np.float32)
    ang = np.asarray(pos, np.float32)[:, None] * inv_freq[None, :]
    return np.cos(ang).astype(np.float32), np.sin(ang).astype(np.float32)


def _l0_consts(w_in, qa_norm, kv_norm, w_qb, w_kvb, q_norm, k_norm, fq_norm, fk_norm, f_bias):
    d = w_in.shape[0]
    z = lambda n: jnp.zeros((d, n), F32)
    cq, ckv = w_in[:, :Q_LORA], w_in[:, Q_LORA:Q_LORA + KV_LORA]
    o = Q_LORA + KV_LORA
    kr = w_in[:, o:o + ROPE_D]
    o += ROPE_D
    fq = w_in[:, o:o + H_B * HEAD_DIM]
    o += H_B * HEAD_DIM
    fk = w_in[:, o:o + KV_B * HEAD_DIM]
    o += KV_B * HEAD_DIM
    fv = w_in[:, o:o + KV_B * HEAD_DIM]
    o += KV_B * HEAD_DIM
    fl = w_in[:, o:o + H_B]
    hr = ROPE_D // 2
    krr = jnp.concatenate([-kr[:, hr:], kr[:, :hr]], axis=1)
    g1 = jnp.concatenate([kr, z(32), kr, z(32)], axis=1)
    g2 = jnp.concatenate([krr, z(32), krr, z(32)], axis=1)
    w_in2 = jnp.concatenate([cq, ckv, g1, g2, fq, fk, fv, fl, z(LANES - H_B)], axis=1).astype(BF16)

    wq = w_qb.reshape(Q_LORA, H_A, NOPE + ROPE_D)
    zq = lambda n: jnp.zeros((Q_LORA, H_A, n), F32)
    rope_c = wq[:, :, NOPE:]
    rope_r = jnp.concatenate([-rope_c[:, :, hr:], rope_c[:, :, :hr]], axis=2)
    wqa = jnp.concatenate([wq[:, :, :NOPE], rope_c, zq(32)], axis=2).reshape(Q_LORA, H_A * LANES)
    wqb = jnp.concatenate([zq(NOPE), rope_r, zq(32)], axis=2).reshape(Q_LORA, H_A * LANES)
    w_q2 = jnp.concatenate([wqa, wqb], axis=1).astype(BF16)

    wkv = w_kvb.reshape(KV_LORA, H_A, NOPE + V_DIM)
    wk = jnp.concatenate([wkv[:, :, :NOPE], jnp.zeros((KV_LORA, H_A, LANES - NOPE), F32)], axis=2)
    w_kv2 = jnp.concatenate([wk.reshape(KV_LORA, H_A * LANES),
                             wkv[:, :, NOPE:].reshape(KV_LORA, H_A * V_DIM)], axis=1).astype(BF16)

    z32 = jnp.zeros((32,), F32)
    qn_r = q_norm[NOPE:]
    ga = jnp.tile(jnp.concatenate([q_norm[:NOPE], qn_r, z32]), H_A) * MLA_SCALE
    gb = jnp.tile(jnp.concatenate([jnp.zeros((NOPE,), F32), qn_r[hr:], qn_r[:hr], z32]), H_A) * MLA_SCALE
    gk = jnp.tile(jnp.concatenate([k_norm[:NOPE], jnp.zeros((LANES - NOPE,), F32)]), H_A)
    kn_r = k_norm[NOPE:]
    kn_rr = jnp.concatenate([kn_r[hr:], kn_r[:hr]])
    gk1 = jnp.concatenate([kn_r, z32, kn_r, z32])
    gk2 = jnp.concatenate([kn_rr, z32, kn_rr, z32])
    vecs = dict(
        qa_norm=qa_norm[None], kv_norm=kv_norm[None], ga=ga[None], gb=gb[None], gk=gk[None],
        gk1=gk1[None], gk2=gk2[None],
        gfq=(jnp.tile(fq_norm, H_B) * HD_SCALE)[None], gfk=jnp.tile(fk_norm, KV_B)[None],
        fbias=jnp.concatenate([f_bias, jnp.zeros((LANES - H_B,), F32)])[None],
    )

    li = np.arange(LANES)
    sq = ((li[:, None] < 64) & (li[None, :] < 64)) | ((li[:, None] >= 64) & (li[:, None] < 96)
                                                       & (li[None, :] >= 64) & (li[None, :] < 96))
    sk = (li[:, None] < 64) & (li[None, :] < 64)
    sf = (li[:, None] // 64) == (li[None, :] // 64)
    cnt_q = np.where(li < 64, 1.0 / 64, np.where(li < 96, 1.0 / 32, 1.0)).astype(np.float32)
    mats = dict(
        s_q=jnp.asarray(sq, BF16), s_k=jnp.asarray(sk, BF16), s_f=jnp.asarray(sf, BF16),
        cnt_q=jnp.asarray(np.tile(cnt_q, H_A))[None],
    )
    return w_in2, w_q2, w_kv2, vecs, mats


def _l0_common(u, wq2_ref, wkv2_ref, p, cq_t, sq_t, ck_t, sk_t):
    cq = u[:, :Q_LORA]
    ckv = u[:, Q_LORA:640]
    g1 = u[:, 640:768]
    g2 = u[:, 768:896]
    fq = u[:, 896:1408]
    fk = u[:, 1408:1664]
    fv = u[:, 1664:1920]
    fl = u[:, 1920:2048]

    cq_n = cq * lax.rsqrt(jnp.mean(cq * cq, axis=-1, keepdims=True) + NORM_EPS) * p["qa_norm"][...]
    q2 = _dot(cq_n, wq2_ref[...])
    qa, qb = q2[:, :H_A * LANES], q2[:, H_A * LANES:]
    rq = _group_rms(qa, p["s_q"][...], p["cnt_q"][...])
    q_mla = rq * (qa * p["ga"][...] * _tile_lanes(cq_t, H_A) + qb * p["gb"][...] * _tile_lanes(sq_t, H_A))

    ckv_n = ckv * lax.rsqrt(jnp.mean(ckv * ckv, axis=-1, keepdims=True) + NORM_EPS) * p["kv_norm"][...]
    kv2 = _dot(ckv_n, wkv2_ref[...])
    kk, v_mla = kv2[:, :H_A * LANES], kv2[:, H_A * LANES:]
    rk = _group_rms(kk, p["s_k"][...], 1.0 / NOPE)
    k_nope = kk * rk * p["gk"][...]

    lane = _lane_iota(g1.shape)
    ss = jnp.sum(jnp.where(lane < ROPE_D, g1 * g1, 0.0), axis=-1, keepdims=True)
    r_kr = lax.rsqrt(ss * (1.0 / ROPE_D) + NORM_EPS)
    kr128 = r_kr * (g1 * p["gk1"][...] * ck_t + g2 * p["gk2"][...] * sk_t)

    rfq = _group_rms(fq, p["s_f"][...], 1.0 / HEAD_DIM)
    fq_n = fq * rfq * p["gfq"][...]
    rfk = _group_rms(fk, p["s_f"][...], 1.0 / HEAD_DIM)
    fk_n = fk * rfk * p["gfk"][...]
    xl = fl + p["fbias"][...]
    logf = jnp.minimum(xl, 0.0) - jnp.log(1.0 + jnp.exp(-jnp.abs(xl)))
    return q_mla, ckv_n, k_nope, v_mla, kr128, fq_n, fk_n, fv, logf


_L0_VEC_NAMES = ("qa_norm", "kv_norm", "ga", "gb", "gk", "gk1", "gk2", "gfq", "gfk", "fbias")
_L0_MAT_NAMES = ("s_q", "s_k", "s_f", "cnt_q")


def _l0_prompt_kernel(tiles_per_seq, x_ref, shift_ref, scale_ref, g_ref, win_ref, wq2_ref, wkv2_ref,
                      cq_ref, sq_ref, ck_ref, sk_ref, eq_ref, ek_ref, ev_ref, pcf_ref, ltri_ref, ones_ref,
                      *rest):
    nv, nm = len(_L0_VEC_NAMES), len(_L0_MAT_NAMES)
    p = dict(zip(_L0_VEC_NAMES + _L0_MAT_NAMES, rest[:nv + nm]))
    q_out, k_out, v_out, ckv_out, kr_out, fkv_out, lf_out, carry = rest[nv + nm:]

    @pl.when(pl.program_id(0) % tiles_per_seq == 0)
    def _():
        carry[...] = jnp.zeros_like(carry)

    xm = _modulate(x_ref[...], g_ref[...], scale_ref[0], shift_ref[0])
    u = _dot(xm, win_ref[...])
    q_mla, ckv_n, k_nope, v_mla, kr128, fq_n, fk_n, fv, logf = _l0_common(
        u, wq2_ref, wkv2_ref, p, cq_ref[...], sq_ref[...], ck_ref[...], sk_ref[...])

    lane = _lane_iota(kr128.shape)
    k_mla = k_nope + _tile_lanes(jnp.where(lane >= NOPE, kr128, 0.0), H_A)

    ltri = ltri_ref[...]
    cf = _dot3_left(ltri, logf) + carry[...]
    carry[...] = cf[cf.shape[0] - 1:, :]
    nh, nm_, nl = _split3(-cf)
    bias = jnp.dot(jnp.concatenate([nh, nm_, nl], axis=1), pcf_ref[...], preferred_element_type=F32)

    q_fox = _dot(fq_n, eq_ref[...]) + ones_ref[...]
    k_fox = _dot(fk_n, ek_ref[...]) + bias
    v_fox = _dot(fv, ev_ref[...])

    q_out[...] = jnp.concatenate([q_mla, q_fox], axis=1).astype(BF16)
    k_out[...] = jnp.concatenate([k_mla, k_fox], axis=1).astype(BF16)
    v_out[...] = jnp.concatenate([v_mla, v_fox], axis=1).astype(BF16)
    ckv_out[...] = ckv_n
    kr_out[...] = kr128[:, :ROPE_D]
    fkv_out[...] = jnp.concatenate([fk_n, fv], axis=1)
    lf_out[...] = logf[:, :H_B]


def _dot3_left(m01, x):
    hi, mid, lo = _split3(x)
    return (jnp.dot(m01, hi, preferred_element_type=F32) + jnp.dot(m01, mid, preferred_element_type=F32)
            + jnp.dot(m01, lo, preferred_element_type=F32))


def _fox_place_mats():
    eq = np.zeros((H_B * HEAD_DIM, H_B * LANES), np.float32)
    ek = np.zeros((KV_B * HEAD_DIM, H_B * LANES), np.float32)
    ev = np.zeros((KV_B * HEAD_DIM, KV_B * LANES), np.float32)
    pcf = np.zeros((3 * LANES, H_B * LANES), np.float32)
    ones = np.zeros((1, H_B * LANES), np.float32)
    dd = np.arange(HEAD_DIM)
    for h in range(H_B):
        eq[h * HEAD_DIM + dd, h * LANES + dd] = 1.0
        ek[(h // 2) * HEAD_DIM + dd, h * LANES + dd] = 1.0
        for s in range(3):
            pcf[s * LANES + h, h * LANES + HEAD_DIM + s] = 1.0
            ones[0, h * LANES + HEAD_DIM + s] = 1.0
    for kvh in range(KV_B):
        for g in range(2):
            ev[kvh * HEAD_DIM + dd, kvh * LANES + g * HEAD_DIM + dd] = 1.0
    return (jnp.asarray(eq, BF16), jnp.asarray(ek, BF16), jnp.asarray(ev, BF16), jnp.asarray(pcf, BF16),
            jnp.asarray(ones, F32))


def _l0_tables(pos):
    c0, s0 = _cos_sin(pos, ROPE_D // 2)
    c = np.concatenate([c0] * 2, axis=1)
    s = np.concatenate([s0] * 2, axis=1)
    n = c.shape[0]
    one = np.ones((n, NOPE), np.float32)
    z32 = np.zeros((n, 32), np.float32)
    z64 = np.zeros((n, NOPE), np.float32)
    cq = np.concatenate([one, c, z32], axis=1)
    sq = np.concatenate([z64, s, z32], axis=1)
    ck = np.concatenate([c, z32, c, z32], axis=1)
    sk = np.concatenate([s, z32, s, z32], axis=1)
    return tuple(jnp.asarray(t) for t in (cq, sq, ck, sk))


def _l0_prompt_proj(x, shift, scale, g, consts, seq):
    n, d = x.shape
    tm = 256
    tps = seq // tm
    w_in2, w_q2, w_kv2, vecs, mats = consts
    cq, sq, ck, sk = _l0_tables(np.arange(seq))
    eq, ek, ev, pcf, ones = _fox_place_mats()
    ltri = jnp.asarray(np.tril(np.ones((tm, tm), np.float32)), BF16)
    tab_spec = pl.BlockSpec((tm, LANES), lambda i: (i % tps, 0))
    row = lambda w: pl.BlockSpec((tm, w), lambda i: (i, 0))
    small = [vecs[k] for k in _L0_VEC_NAMES] + [mats[k] for k in _L0_MAT_NAMES]
    ins = [x, shift, scale, g, w_in2, w_q2, w_kv2, cq, sq, ck, sk, eq, ek, ev, pcf, ltri, ones] + small
    in_specs = ([row(d), _mod_spec(shift, tps), _mod_spec(scale, tps), _const_spec(g), _const_spec(w_in2),
                 _const_spec(w_q2), _const_spec(w_kv2), tab_spec, tab_spec, tab_spec, tab_spec,
                 _const_spec(eq), _const_spec(ek), _const_spec(ev), _const_spec(pcf), _const_spec(ltri),
                 _const_spec(ones)] + [_const_spec(a) for a in small])
    widths = (2 * H_A * LANES, 2 * H_A * LANES, H_A * V_DIM + KV_B * LANES, KV_LORA, ROPE_D,
              2 * KV_B * HEAD_DIM, H_B)
    dtypes = (BF16, BF16, BF16, F32, F32, F32, F32)
    return pl.pallas_call(
        functools.partial(_l0_prompt_kernel, tps),
        grid=(n // tm,),
        in_specs=in_specs,
        out_specs=[row(w) for w in widths],
        out_shape=[jax.ShapeDtypeStruct((n, w), dt) for w, dt in zip(widths, dtypes)],
        scratch_shapes=[pltpu.VMEM((1, LANES), F32)],
        compiler_params=_cparams(("arbitrary",)),
        name="l0_prompt_proj",
    )(*ins)


def _l0_sample_kernel(x_ref, shift_ref, scale_ref, g_ref, win_ref, wq2_ref, wkv2_ref,
                      cq_ref, sq_ref, ck_ref, sk_ref, *rest):
    nv, nm = len(_L0_VEC_NAMES), len(_L0_MAT_NAMES)
    p = dict(zip(_L0_VEC_NAMES + _L0_MAT_NAMES, rest[:nv + nm]))
    q_out, ckv_out, kr_out, fq_out, fkv_out, lf_out = rest[nv + nm:]
    xm = _modulate(x_ref[...], g_ref[...], scale_ref[0], shift_ref[0])
    u = _dot(xm, win_ref[...])
    q_mla, ckv_n, _, _, kr128, fq_n, fk_n, fv, logf = _l0_common(
        u, wq2_ref, wkv2_ref, p, cq_ref[...], sq_ref[...], ck_ref[...], sk_ref[...])
    q_out[...] = q_mla
    ckv_out[...] = ckv_n
    kr_out[...] = kr128
    fq_out[...] = fq_n
    fkv_out[...] = jnp.concatenate([fk_n, fv], axis=1)
    lf_out[...] = logf


def _l0_sample_proj(x, shift, scale, g, consts, past):
    n, d = x.shape
    w_in2, w_q2, w_kv2, vecs, mats = consts
    tabs = _l0_tables(np.full((1,), past))
    small = [vecs[k] for k in _L0_VEC_NAMES] + [mats[k] for k in _L0_MAT_NAMES]
    ins = [x, shift, scale, g, w_in2, w_q2, w_kv2, *tabs] + small
    widths = (H_A * LANES, KV_LORA, LANES, H_B * HEAD_DIM, 2 * KV_B * HEAD_DIM, LANES)
    return pl.pallas_call(
        _l0_sample_kernel,
        grid=(1,),
        in_specs=[_const_spec(a) for a in ins],
        out_specs=[pl.BlockSpec((n, w), lambda i: (0, 0)) for w in widths],
        out_shape=[jax.ShapeDtypeStruct((n, w), F32) for w in widths],
        compiler_params=_cparams(("arbitrary",)),
        name="l0_sample_proj",
    )(*ins)


def _flash_kernel(tq, qi_ref, kj_ref, q_ref, k_ref, v_ref, o_ref, m_scr, l_scr, acc_scr):
    t = pl.program_id(2)
    i, j = qi_ref[t], kj_ref[t]
    tk = tq

    @pl.when(j == 0)
    def _():
        m_scr[...] = jnp.full(m_scr.shape, NEG, F32)
        l_scr[...] = jnp.zeros_like(l_scr)
        acc_scr[...] = jnp.zeros_like(acc_scr)

    def step(masked):
        q = q_ref[0]
        k = k_ref[0]
        v = v_ref[0]
        lane = _lane_iota((tq, LANES))
        new_acc = []
        for h in range(2):
            s = _dot_nt(q[:, h * LANES:(h + 1) * LANES], k[:, h * LANES:(h + 1) * LANES])
            if masked:
                row = lax.broadcasted_iota(I32, (tq, tk), 0)
                col = lax.broadcasted_iota(I32, (tq, tk), 1)
                s = jnp.where(col <= row, s, NEG)
            m_prev = m_scr[h]
            m_new = jnp.maximum(m_prev, jnp.max(s, axis=-1, keepdims=True))
            alpha = jnp.exp(m_prev - m_new)
            pr = jnp.exp(s - m_new[:, :1])
            l_scr[h] = alpha * l_scr[h] + jnp.sum(pr, axis=-1, keepdims=True)
            m_scr[h] = m_new
            new_acc.append(alpha * acc_scr[...] + _dot(pr, v))
        acc_scr[...] = jnp.where(lane < V_DIM, new_acc[0], new_acc[1])

    @pl.when(j < i)
    def _():
        step(False)

    @pl.when(j == i)
    def _():
        step(True)
        lane = _lane_iota((tq, LANES))
        l = jnp.where(lane < V_DIM, l_scr[0], l_scr[1])
        o_ref[0] = (acc_scr[...] / l).astype(o_ref.dtype)


def _flash_attention(q, k, v):
    b, s, _ = q.shape
    tq = FLASH_TILE if s % FLASH_TILE == 0 else 512
    nq = s // tq
    npair = H_A // 2 + H_B // 2
    qi = np.concatenate([np.full(i + 1, i) for i in range(nq)]).astype(np.int32)
    kj = np.concatenate([np.arange(i + 1) for i in range(nq)]).astype(np.int32)
    gs = pltpu.PrefetchScalarGridSpec(
        num_scalar_prefetch=2,
        grid=(b, npair, qi.shape[0]),
        in_specs=[pl.BlockSpec((1, tq, 2 * LANES), lambda bb, p, t, qi_, kj_: (bb, qi_[t], p)),
                  pl.BlockSpec((1, tq, 2 * LANES), lambda bb, p, t, qi_, kj_: (bb, kj_[t], p)),
                  pl.BlockSpec((1, tq, LANES), lambda bb, p, t, qi_, kj_: (bb, kj_[t], p))],
        out_specs=pl.BlockSpec((1, tq, LANES), lambda bb, p, t, qi_, kj_: (bb, qi_[t], p)),
        scratch_shapes=[pltpu.VMEM((2, tq, LANES), F32), pltpu.VMEM((2, tq, LANES), F32),
                        pltpu.VMEM((tq, LANES), F32)],
    )
    return pl.pallas_call(
        functools.partial(_flash_kernel, tq),
        grid_spec=gs,
        out_shape=jax.ShapeDtypeStruct((b, s, npair * LANES), BF16),
        compiler_params=_cparams(("arbitrary",) * 3),
        name="l0_flash",
    )(jnp.asarray(qi), jnp.asarray(kj), q, k, v)


def _router_epilogue(h1, g2, shift2, scale2, wr_ref, br_ref, xm_ref, gate_ref, eidx_ref):
    tm = h1.shape[0]
    xm = _modulate(h1, g2, scale2, shift2)
    for c in range(ROW_TILE_CHUNKS):
        xm_ref[pl.ds(c, tm, stride=SUBLANES), :] = xm[:, c * LANES:(c + 1) * LANES]
    logits = jnp.dot(xm, wr_ref[...], preferred_element_type=F32, precision=lax.Precision.HIGHEST) + br_ref[...]
    lane = _lane_iota(logits.shape)
    x = logits
    vals, ev = [], jnp.zeros(logits.shape, I32)
    for kk in range(TOP_K):
        m = jnp.max(x, axis=-1, keepdims=True)
        idx = jnp.min(jnp.where(x == m, lane, LANES), axis=-1, keepdims=True)
        vals.append(m)
        ev = jnp.where(lane == kk, idx, ev)
        x = jnp.where(lane == idx, -3e38, x)
    es = [jnp.exp(vv - vals[0]) for vv in vals]
    tot = es[0] + es[1] + es[2] + es[3]
    gv = jnp.zeros(logits.shape, F32)
    for kk in range(TOP_K):
        gv = jnp.where(lane == kk, es[kk] / tot, gv)
    gate_ref[...] = gv
    eidx_ref[...] = ev


def _out_router_kernel(a_ref, wo_ref, res_ref, gate1_ref, g2_ref, shift2_ref, scale2_ref, wr_ref, br_ref,
                       h1_ref, xm_ref, gate_ref, eidx_ref):
    h1 = res_ref[...] + gate1_ref[0] * _dot(a_ref[...], wo_ref[...])
    h1_ref[...] = h1
    _router_epilogue(h1, g2_ref[...], shift2_ref[0], scale2_ref[0], wr_ref, br_ref, xm_ref, gate_ref, eidx_ref)


def _router_consts(router_w, router_b):
    d, e = router_w.shape
    wr = jnp.concatenate([router_w, jnp.zeros((d, LANES - e), F32)], axis=1)
    br = jnp.concatenate([router_b, jnp.full((LANES - e,), NEG, F32)])[None]
    return wr, br


def _router_out_specs(n, tm):
    specs = [pl.BlockSpec((tm, D_MODEL), lambda i: (i, 0)),
             pl.BlockSpec((tm * SUBLANES, LANES), lambda i: (i, 0)),
             pl.BlockSpec((tm, LANES), lambda i: (i, 0)),
             pl.BlockSpec((tm, LANES), lambda i: (i, 0))]
    shapes = [jax.ShapeDtypeStruct((n, D_MODEL), F32), jax.ShapeDtypeStruct((n * SUBLANES, LANES), F32),
              jax.ShapeDtypeStruct((n, LANES), F32), jax.ShapeDtypeStruct((n, LANES), I32)]
    return specs, shapes


def _out_router(a, w_out, res, gate1, g2, shift2, scale2, wr, br, rows_per_group):
    n, ka = a.shape
    tm = min(256, n)
    tpg = max(rows_per_group // tm, 1)
    wo = w_out.astype(BF16)
    out_specs, out_shapes = _router_out_specs(n, tm)
    return pl.pallas_call(
        _out_router_kernel,
        grid=(n // tm,),
        in_specs=[pl.BlockSpec((tm, ka), lambda i: (i, 0)), _const_spec(wo),
                  pl.BlockSpec((tm, D_MODEL), lambda i: (i, 0)), _mod_spec(gate1, tpg), _const_spec(g2),
                  _mod_spec(shift2, tpg), _mod_spec(scale2, tpg), _const_spec(wr), _const_spec(br)],
        out_specs=out_specs,
        out_shape=out_shapes,
        compiler_params=_cparams(("arbitrary",)),
        name="out_router",
    )(a, wo, res, gate1, g2, shift2, scale2, wr, br)


def _moe_plan(eidx, tb):
    n = eidx.shape[0]
    m = n * TOP_K
    flat_e = eidx[:, :TOP_K].reshape(m)
    onehot = (flat_e[:, None] == jnp.arange(N_EXPERTS, dtype=I32)[None, :]).astype(I32)
    csum = jnp.cumsum(onehot, axis=0)
    counts = csum[-1]
    padded = (counts + tb - 1) // tb * tb
    pend = jnp.cumsum(padded)
    pstart = pend - padded
    pos = jnp.sum(onehot * (csum - 1 + pstart[None, :]), axis=1)
    nblk = -(-m // tb) + N_EXPERTS
    first_row = jnp.arange(nblk, dtype=I32) * tb
    blk_e = jnp.minimum(jnp.sum((pend[None, :] <= first_row[:, None]).astype(I32), axis=1), N_EXPERTS - 1)
    row_tok = jnp.zeros((nblk * tb,), I32).at[pos].set(jnp.arange(m, dtype=I32) // TOP_K)
    return pos.astype(I32), blk_e, row_tok, nblk


def _gather_rows(idx_ref, src_hbm, dst, sem, nrow):
    def body(g, c):
        for u in range(DMA_UNROLL):
            r = g * DMA_UNROLL + u
            t = idx_ref[0, 0, r]
            pltpu.make_async_copy(src_hbm.at[pl.ds(t * SUBLANES, SUBLANES)],
                                  dst.at[pl.ds(r * SUBLANES, SUBLANES)], sem).start()
        return c
    lax.fori_loop(0, nrow // DMA_UNROLL, body, 0)


def _wait_rows(src_hbm, dst, sem):
    pltpu.make_async_copy(src_hbm.at[pl.ds(0, dst.shape[0])], dst, sem).wait()


def _expert_kernel(tb, blk_e_ref, tok0_ref, tokn_ref, x_hbm, w1_ref, b1_ref, w2_ref, b2_ref, y_ref, xg0, xg1,
                   sem):
    del blk_e_ref
    i, nb = pl.program_id(0), pl.num_programs(0)
    bufs = (xg0, xg1)

    @pl.when(i == 0)
    def _():
        _gather_rows(tok0_ref, x_hbm, xg0, sem.at[0], tb)

    def run(slot):
        @pl.when(i + 1 < nb)
        def _():
            _gather_rows(tokn_ref, x_hbm, bufs[1 - slot], sem.at[1 - slot], tb)

        xg = bufs[slot]
        _wait_rows(x_hbm, xg, sem.at[slot])
        x = jnp.concatenate([xg[pl.ds(c, tb, stride=SUBLANES), :] for c in range(ROW_TILE_CHUNKS)], axis=1)
        hcat = _dot(x, w1_ref[0, 0]) + b1_ref[0, 0]
        glu = jnp.minimum(hcat[:, :D_FF], SWIGLU_LIMIT)
        lin = jnp.clip(hcat[:, D_FF:], -SWIGLU_LIMIT, SWIGLU_LIMIT)
        act = glu * jax.nn.sigmoid(SWIGLU_ALPHA * glu) * (lin + 1.0)
        y = _dot(act, w2_ref[0, 0]) + b2_ref[0, 0]
        for c in range(ROW_TILE_CHUNKS):
            y_ref[pl.ds(c, tb, stride=SUBLANES), :] = y[:, c * LANES:(c + 1) * LANES]

    @pl.when(i % 2 == 0)
    def _():
        run(0)

    @pl.when(i % 2 == 1)
    def _():
        run(1)


def _experts(xm_tiles, blk_e, row_tok, nblk, tb, layer, w1, b1, w2, b2):
    nxt = lambda i, be: (jnp.minimum(i + 1, nblk - 1), 0, 0)
    gs = pltpu.PrefetchScalarGridSpec(
        num_scalar_prefetch=1,
        grid=(nblk,),
        in_specs=[pl.BlockSpec((1, 1, tb), lambda i, be: (0, 0, 0), memory_space=pltpu.SMEM),
                  pl.BlockSpec((1, 1, tb), nxt, memory_space=pltpu.SMEM),
                  pl.BlockSpec(memory_space=pl.ANY),
                  pl.BlockSpec((1, 1, D_MODEL, 2 * D_FF), lambda i, be: (layer, be[i], 0, 0)),
                  pl.BlockSpec((1, 1, 1, 2 * D_FF), lambda i, be: (layer, be[i], 0, 0)),
                  pl.BlockSpec((1, 1, D_FF, D_MODEL), lambda i, be: (layer, be[i], 0, 0)),
                  pl.BlockSpec((1, 1, 1, D_MODEL), lambda i, be: (layer, be[i], 0, 0))],
        out_specs=pl.BlockSpec((tb * SUBLANES, LANES), lambda i, be: (i, 0)),
        scratch_shapes=[pltpu.VMEM((tb * SUBLANES, LANES), F32), pltpu.VMEM((tb * SUBLANES, LANES), F32),
                        pltpu.SemaphoreType.DMA((2,))],
    )
    depth, ne = b1.shape[:2]
    tok3 = row_tok.reshape(nblk, 1, tb)
    return pl.pallas_call(
        functools.partial(_expert_kernel, tb),
        grid_spec=gs,
        out_shape=jax.ShapeDtypeStruct((nblk * tb * SUBLANES, LANES), F32),
        compiler_params=_cparams(("arbitrary",)),
        name="moe_experts",
    )(blk_e, tok3, tok3, xm_tiles, w1, b1.reshape(depth, ne, 1, 2 * D_FF), w2,
      b2.reshape(depth, ne, 1, D_MODEL))


def _combine_kernel(tc, pos0_ref, posn_ref, y_hbm, gates_ref, h1_ref, gate2_ref, o_ref, buf0, buf1, sem):
    nrow = TOP_K * tc
    i, nb = pl.program_id(0), pl.num_programs(0)
    bufs = (buf0, buf1)

    @pl.when(i == 0)
    def _():
        _gather_rows(pos0_ref, y_hbm, buf0, sem.at[0], nrow)

    def run(slot):
        @pl.when(i + 1 < nb)
        def _():
            _gather_rows(posn_ref, y_hbm, bufs[1 - slot], sem.at[1 - slot], nrow)

        buf = bufs[slot]
        _wait_rows(y_hbm, buf, sem.at[slot])
        gates = gates_ref[...]
        cols = []
        for c in range(ROW_TILE_CHUNKS):
            acc = jnp.zeros((tc, LANES), F32)
            for kk in range(TOP_K):
                yk = buf[pl.ds(kk * tc * SUBLANES + c, tc, stride=SUBLANES), :]
                acc = acc + yk * gates[:, kk:kk + 1]
            cols.append(acc)
        moe = jnp.concatenate(cols, axis=1)
        o_ref[...] = h1_ref[...] + gate2_ref[0] * moe

    @pl.when(i % 2 == 0)
    def _():
        run(0)

    @pl.when(i % 2 == 1)
    def _():
        run(1)


def _combine(pos, ys, gates, h1, gate2, rows_per_group):
    n = h1.shape[0]
    tc = min(256, n)
    nt = n // tc
    tpg = max(rows_per_group // tc, 1)
    pos_blk = pos.reshape(nt, tc, TOP_K).transpose(0, 2, 1).reshape(nt, 1, TOP_K * tc)
    nbuf = TOP_K * tc * SUBLANES
    return pl.pallas_call(
        functools.partial(_combine_kernel, tc),
        grid=(nt,),
        in_specs=[pl.BlockSpec((1, 1, TOP_K * tc), lambda i: (0, 0, 0), memory_space=pltpu.SMEM),
                  pl.BlockSpec((1, 1, TOP_K * tc), lambda i: (jnp.minimum(i + 1, nt - 1), 0, 0),
                               memory_space=pltpu.SMEM),
                  pl.BlockSpec(memory_space=pl.ANY),
                  pl.BlockSpec((tc, LANES), lambda i: (i, 0)),
                  pl.BlockSpec((tc, D_MODEL), lambda i: (i, 0)),
                  _mod_spec(gate2, tpg)],
        out_specs=pl.BlockSpec((tc, D_MODEL), lambda i: (i, 0)),
        out_shape=jax.ShapeDtypeStruct((n, D_MODEL), F32),
        scratch_shapes=[pltpu.VMEM((nbuf, LANES), F32), pltpu.VMEM((nbuf, LANES), F32),
                        pltpu.SemaphoreType.DMA((2,))],
        compiler_params=_cparams(("arbitrary",)),
        name="moe_combine",
    )(pos_blk, pos_blk, ys, gates, h1, gate2)


def _moe(h1, xm_tiles, gates, eidx, gate2, rows_per_group, layer, w1, b1, w2, b2):
    n = h1.shape[0]
    tb = MOE_ROWS if n * TOP_K >= 16 * MOE_ROWS else PAGE
    pos, blk_e, row_tok, nblk = _moe_plan(eidx, tb)
    ys = _experts(xm_tiles, blk_e, row_tok, nblk, tb, layer, w1, b1, w2, b2)
    return _combine(pos, ys, gates, h1, gate2, rows_per_group)


PAGES_PER_STEP = 32
CUMSUM_CHUNK = 1024


def _softmax_step(s, m_scr, l_scr):
    m_prev = m_scr[...]
    m_new = jnp.maximum(m_prev, jnp.max(s, axis=-1, keepdims=True))
    alpha = jnp.exp(m_prev - m_new)
    pr = jnp.exp(s - m_new[:, :1])
    l_scr[...] = alpha * l_scr[...] + jnp.sum(pr, axis=-1, keepdims=True)
    m_scr[...] = m_new
    return alpha[:, :1], pr


def _mla_decode_kernel(npg, pt_ref, q_ref, cnew_ref, krnew_ref, gk_ref, t64_ref, mask_ref, wkbt_ref, wvb_ref,
                       *rest):
    ckv_refs, kr_refs = rest[:npg], rest[npg:2 * npg]
    o_ref, m_scr, l_scr, acc_scr = rest[2 * npg:]
    del pt_ref
    c = pl.program_id(1)

    @pl.when(c == 0)
    def _():
        m_scr[...] = jnp.full(m_scr.shape, NEG, F32)
        l_scr[...] = jnp.zeros_like(l_scr)
        acc_scr[...] = jnp.zeros_like(acc_scr)

    q8 = q_ref[0]
    mask = mask_ref[...]
    qmat = _dot(q8[:, :NOPE] * gk_ref[...], t64_ref[...]) * mask
    qr = q8[:, NOPE:NOPE + ROPE_D]
    cc = jnp.concatenate([r[0] for r in ckv_refs], axis=0).astype(BF16)
    kt = _dot_nt(wkbt_ref[...], cc)
    ssq = _dot(mask, kt * kt)
    tt = _dot(qmat, kt)
    krt = jnp.concatenate([r[0] for r in kr_refs], axis=1)
    s = tt * lax.rsqrt(ssq * (1.0 / NOPE) + NORM_EPS) + _dot(qr, krt)
    alpha, pr = _softmax_step(s, m_scr, l_scr)
    acc_scr[...] = alpha * acc_scr[...] + _dot(pr, cc)

    @pl.when(c == pl.num_programs(1) - 1)
    def _():
        cnew = cnew_ref[0]
        knew = _dot_nt(jnp.broadcast_to(cnew, (SUBLANES, KV_LORA)), wkbt_ref[...])
        tt_n = jnp.sum(qmat * knew, axis=-1, keepdims=True)
        ssq_n = jnp.sum(mask * knew * knew, axis=-1, keepdims=True)
        s_n = (tt_n * lax.rsqrt(ssq_n * (1.0 / NOPE) + NORM_EPS)
               + jnp.sum(qr * krnew_ref[0][:, :ROPE_D], axis=-1, keepdims=True))
        alpha_n, p_n = _softmax_step(s_n, m_scr, l_scr)
        acc = alpha_n * acc_scr[...] + p_n * cnew
        lat = acc / l_scr[...][:, :1]
        o8 = _dot(lat, wvb_ref[...]) * mask
        o_ref[0] = jnp.sum(o8, axis=0, keepdims=True)


def _mla_decode(page_table, q_s, ckv_s, kr_s, ckv_cache, krt_cache, w_kvb, k_norm):
    bd, n_pages = page_table.shape
    npg = PAGES_PER_STEP
    wkv = w_kvb.reshape(KV_LORA, H_A, NOPE + V_DIM)
    wkbt = wkv[:, :, :NOPE].reshape(KV_LORA, H_A * NOPE).T.astype(BF16)
    wvb = wkv[:, :, NOPE:].reshape(KV_LORA, H_A * V_DIM).astype(BF16)
    hh = np.arange(H_A)[:, None]
    mask = jnp.asarray((np.arange(H_A * NOPE)[None, :] // NOPE) == hh, F32)
    t64 = jnp.asarray(np.tile(np.eye(NOPE, dtype=np.float32), (1, H_A)), BF16)
    gk = k_norm[:NOPE][None]
    q8 = q_s.reshape(bd, H_A, LANES)
    consts = [gk, t64, mask, wkbt, wvb]

    def page_spec(shape, k):
        return pl.BlockSpec((1,) + shape, lambda b, c, pt: (pt[b, c * npg + k], 0, 0))

    gs = pltpu.PrefetchScalarGridSpec(
        num_scalar_prefetch=1,
        grid=(bd, n_pages // npg),
        in_specs=([pl.BlockSpec((1, H_A, LANES), lambda b, c, pt: (b, 0, 0)),
                   pl.BlockSpec((1, 1, KV_LORA), lambda b, c, pt: (b, 0, 0)),
                   pl.BlockSpec((1, 1, LANES), lambda b, c, pt: (b, 0, 0))]
                  + [pl.BlockSpec(a.shape, lambda b, c, pt: (0, 0)) for a in consts]
                  + [page_spec((PAGE, KV_LORA), k) for k in range(npg)]
                  + [page_spec((ROPE_D, PAGE), k) for k in range(npg)]),
        out_specs=pl.BlockSpec((1, 1, H_A * V_DIM), lambda b, c, pt: (b, 0, 0)),
        scratch_shapes=[pltpu.VMEM((SUBLANES, LANES), F32), pltpu.VMEM((SUBLANES, LANES), F32),
                        pltpu.VMEM((SUBLANES, KV_LORA), F32)],
    )
    out = pl.pallas_call(
        functools.partial(_mla_decode_kernel, npg),
        grid_spec=gs,
        out_shape=jax.ShapeDtypeStruct((bd, 1, H_A * V_DIM), F32),
        compiler_params=_cparams(("arbitrary", "arbitrary")),
        name="l0_mla_decode",
    )(page_table, q8, ckv_s.reshape(bd, 1, KV_LORA), kr_s.reshape(bd, 1, LANES), *consts,
      *([ckv_cache] * npg), *([krt_cache] * npg))
    return out.reshape(bd, H_A * V_DIM)


def _fox_decode_kernel(npg, pt_ref, q_ref, knew_ref, vnew_ref, lfnew_ref, t64_ref, mask_ref, utri_ref,
                       pe_ref, po_ref, *rest):
    kv_refs, lf_refs = rest[:npg], rest[npg:2 * npg]
    o_ref, m_scr, l_scr, acc_scr, carry = rest[2 * npg:]
    del pt_ref
    c = pl.program_id(1)
    nkv = KV_B * HEAD_DIM

    @pl.when(c == 0)
    def _():
        m_scr[...] = jnp.full(m_scr.shape, NEG, F32)
        l_scr[...] = jnp.zeros_like(l_scr)
        acc_scr[...] = jnp.zeros_like(acc_scr)
        carry[...] = jnp.zeros_like(carry)

    mask = mask_ref[...]
    qblk = _dot(q_ref[0], t64_ref[...]) * mask
    kt = jnp.concatenate([r[0, :nkv, :] for r in kv_refs], axis=1)
    vt = jnp.concatenate([r[0, nkv:, :] for r in kv_refs], axis=1)
    lft = jnp.concatenate([r[0] for r in lf_refs], axis=1)
    run = carry[...][:, :1]
    cfs = []
    for sc in range(lft.shape[1] // CUMSUM_CHUNK):
        hi, mid, lo = _split3(lft[:, sc * CUMSUM_CHUNK:(sc + 1) * CUMSUM_CHUNK])
        cf3 = jnp.dot(jnp.concatenate([hi, mid, lo], axis=0), utri_ref[...], preferred_element_type=F32)
        cfs.append(cf3[:H_B] + cf3[H_B:2 * H_B] + cf3[2 * H_B:] + run)
        run = cfs[-1][:, CUMSUM_CHUNK - 1:]
    cf = cfs[0] if len(cfs) == 1 else jnp.concatenate(cfs, axis=1)
    carry[...] = jnp.broadcast_to(run, carry.shape)
    s = _dot(qblk, kt) - cf
    alpha, pr = _softmax_step(s, m_scr, l_scr)
    acc_scr[...] = alpha * acc_scr[...] + _dot_nt(pr, vt)

    @pl.when(c == pl.num_programs(1) - 1)
    def _():
        cf_t = carry[...][:, :1] + lfnew_ref[0][:, :1]
        s_n = jnp.sum(qblk * knew_ref[0], axis=-1, keepdims=True) - cf_t
        alpha_n, p_n = _softmax_step(s_n, m_scr, l_scr)
        acc = (alpha_n * acc_scr[...] + p_n * vnew_ref[0]) / l_scr[...][:, :1] * mask
        row = lax.broadcasted_iota(I32, acc.shape, 0)
        even = jnp.where(row % 2 == 0, acc, 0.0)
        odd = jnp.where(row % 2 == 1, acc, 0.0)
        o8 = _dot(even, pe_ref[...]) + _dot(odd, po_ref[...])
        o_ref[0] = jnp.sum(o8, axis=0, keepdims=True)


def _fox_decode(page_table, fq_s, fkv_s, lf_s, kvt_cache, lft_cache):
    bd, n_pages = page_table.shape
    npg = PAGES_PER_STEP
    nkv = KV_B * HEAD_DIM
    pc = npg * PAGE
    hh = np.arange(H_B)[:, None]
    mask = jnp.asarray((np.arange(nkv)[None, :] // HEAD_DIM) == hh // 2, F32)
    t64 = jnp.asarray(np.tile(np.eye(HEAD_DIM, dtype=np.float32), (1, KV_B)), BF16)
    assert pc % CUMSUM_CHUNK == 0
    utri = jnp.asarray(np.triu(np.ones((CUMSUM_CHUNK, CUMSUM_CHUNK), np.float32)), BF16)
    pe = np.zeros((nkv, H_B * HEAD_DIM), np.float32)
    po = np.zeros((nkv, H_B * HEAD_DIM), np.float32)
    dd = np.arange(HEAD_DIM)
    for j in range(KV_B):
        pe[j * HEAD_DIM + dd, (2 * j) * HEAD_DIM + dd] = 1.0
        po[j * HEAD_DIM + dd, (2 * j + 1) * HEAD_DIM + dd] = 1.0
    consts = [t64, mask, utri, jnp.asarray(pe, BF16), jnp.asarray(po, BF16)]
    lfnew = jnp.broadcast_to(lf_s[:, :H_B, None], (bd, H_B, LANES))

    def page_spec(shape, k):
        return pl.BlockSpec((1,) + shape, lambda b, c, pt: (pt[b, c * npg + k], 0, 0))

    gs = pltpu.PrefetchScalarGridSpec(
        num_scalar_prefetch=1,
        grid=(bd, n_pages // npg),
        in_specs=([pl.BlockSpec((1, H_B, HEAD_DIM), lambda b, c, pt: (b, 0, 0)),
                   pl.BlockSpec((1, 1, nkv), lambda b, c, pt: (b, 0, 0)),
                   pl.BlockSpec((1, 1, nkv), lambda b, c, pt: (b, 0, 0)),
                   pl.BlockSpec((1, H_B, LANES), lambda b, c, pt: (b, 0, 0))]
                  + [pl.BlockSpec(a.shape, lambda b, c, pt: (0, 0)) for a in consts]
                  + [page_spec((2 * nkv, PAGE), k) for k in range(npg)]
                  + [page_spec((H_B, PAGE), k) for k in range(npg)]),
        out_specs=pl.BlockSpec((1, 1, H_B * HEAD_DIM), lambda b, c, pt: (b, 0, 0)),
        scratch_shapes=[pltpu.VMEM((SUBLANES, LANES), F32), pltpu.VMEM((SUBLANES, LANES), F32),
                        pltpu.VMEM((SUBLANES, nkv), F32), pltpu.VMEM((SUBLANES, LANES), F32)],
    )
    out = pl.pallas_call(
        functools.partial(_fox_decode_kernel, npg),
        grid_spec=gs,
        out_shape=jax.ShapeDtypeStruct((bd, 1, H_B * HEAD_DIM), F32),
        compiler_params=_cparams(("arbitrary", "arbitrary")),
        name="l0_fox_decode",
    )(page_table, fq_s.reshape(bd, H_B, HEAD_DIM), fkv_s[:, :nkv].reshape(bd, 1, nkv),
      fkv_s[:, nkv:].reshape(bd, 1, nkv), lfnew, *consts, *([kvt_cache] * npg), *([lft_cache] * npg))
    return out.reshape(bd, H_B * HEAD_DIM)


N_GROUPS = len(C_WINDOWS)
GROUP_COLS = 3 * H_C * HEAD_DIM
HC_COLS = H_C * HEAD_DIM


def _rot_half64(x):
    n = x.shape[1]
    lane = _lane_iota(x.shape)
    fwd = pltpu.roll(x, n - HEAD_DIM // 2, axis=1)
    bwd = pltpu.roll(x, HEAD_DIM // 2, axis=1)
    return jnp.where(lane % HEAD_DIM < HEAD_DIM // 2, -fwd, bwd)


def _l1_qkv(u, g, s_f, gq, gk, cos, sin):
    base = g * GROUP_COLS
    q = u[:, base:base + HC_COLS]
    k = u[:, base + HC_COLS:base + 2 * HC_COLS]
    v = u[:, base + 2 * HC_COLS:base + 3 * HC_COLS]
    qn = q * _group_rms(q, s_f, 1.0 / HEAD_DIM) * gq
    kn = k * _group_rms(k, s_f, 1.0 / HEAD_DIM) * gk
    qn = qn * cos + _rot_half64(qn) * sin
    kn = kn * cos + _rot_half64(kn) * sin
    return qn, kn, v


def _l1_tables(pos):
    c, s = _cos_sin(pos, HEAD_DIM // 2)
    return jnp.asarray(np.concatenate([c] * 4, axis=1)), jnp.asarray(np.concatenate([s] * 4, axis=1))


def _l1_prompt_kernel(tm, x_ref, shift_ref, scale_ref, g_ref, w_ref, sf_ref, gq_ref, gk_ref, cos_ref, sin_ref,
                      *rest):
    outs, kvlast_ref, scr = rest[:3 * N_GROUPS], rest[3 * N_GROUPS], rest[3 * N_GROUPS + 1]
    xm = _modulate(x_ref[...], g_ref[...], scale_ref[0], shift_ref[0])
    u = _dot(xm, w_ref[...])
    cos = _tile_lanes(cos_ref[...], HC_COLS // LANES)
    sin = _tile_lanes(sin_ref[...], HC_COLS // LANES)
    for g in range(N_GROUPS):
        d = C_DILATIONS[g]
        qn, kn, v = _l1_qkv(u, g, sf_ref[...], gq_ref[...], gk_ref[...], cos, sin)
        kvlast_ref[0, :, g * 2 * HC_COLS:g * 2 * HC_COLS + HC_COLS] = kn
        kvlast_ref[0, :, g * 2 * HC_COLS + HC_COLS:(g + 1) * 2 * HC_COLS] = v
        for t, val in enumerate((qn, kn, v)):
            o_ref = outs[3 * g + t]
            if d == 1:
                o_ref[0, 0] = val.astype(BF16)
            else:
                for cc in range(HC_COLS // LANES):
                    scr[cc] = val[:, cc * LANES:(cc + 1) * LANES]
                for r in range(d):
                    o_ref[0, r] = jnp.concatenate(
                        [scr[cc, pl.ds(r, tm // d, stride=d), :] for cc in range(HC_COLS // LANES)],
                        axis=1).astype(BF16)


def _l1_prompt_proj(x, shift, scale, g, w_in, q_norm, k_norm, batch, seq):
    n, dm = x.shape
    tm = 256
    tps = seq // tm
    wmax = max(C_WINDOWS)
    assert seq >= wmax and wmax % tm == 0
    w = w_in.astype(BF16)
    li = np.arange(LANES)
    sf = jnp.asarray((li[:, None] // HEAD_DIM) == (li[None, :] // HEAD_DIM), BF16)
    gq = (jnp.tile(q_norm, H_C) * HD_SCALE)[None]
    gk = jnp.tile(k_norm, H_C)[None]
    cos, sin = _l1_tables(np.arange(seq))
    tab = pl.BlockSpec((tm, LANES), lambda i: (i % tps, 0))
    first_kept = tps - wmax // tm
    out_specs, out_shapes = [], []
    for gi in range(N_GROUPS):
        d = C_DILATIONS[gi]
        for _ in range(3):
            out_specs.append(pl.BlockSpec((1, d, tm // d, HC_COLS), lambda i: (i // tps, 0, i % tps, 0)))
            out_shapes.append(jax.ShapeDtypeStruct((batch, d, seq // d, HC_COLS), BF16))
    out_specs.append(pl.BlockSpec((1, tm, 2 * N_GROUPS * HC_COLS),
                                  lambda i: (i // tps, jnp.maximum(i % tps - first_kept, 0), 0)))
    out_shapes.append(jax.ShapeDtypeStruct((batch, wmax, 2 * N_GROUPS * HC_COLS), F32))
    ins = [x, shift, scale, g, w, sf, gq, gk, cos, sin]
    return pl.pallas_call(
        functools.partial(_l1_prompt_kernel, tm),
        grid=(n // tm,),
        in_specs=[pl.BlockSpec((tm, dm), lambda i: (i, 0)), _mod_spec(shift, tps), _mod_spec(scale, tps),
                  _const_spec(g), _const_spec(w), _const_spec(sf), _const_spec(gq), _const_spec(gk), tab, tab],
        out_specs=out_specs,
        out_shape=out_shapes,
        scratch_shapes=[pltpu.VMEM((HC_COLS // LANES, tm, LANES), F32)],
        compiler_params=_cparams(("arbitrary",)),
        name="l1_prompt_proj",
    )(*ins)


def _dsa_kernel(bpc, q_ref, kc_ref, kp_ref, vc_ref, vp_ref, num_ref, m_ref, den_ref):
    n = pl.program_id(1)
    tq = q_ref.shape[1]
    lo = jnp.where(n % bpc == 0, tq, 0)
    q = q_ref[0]
    kcat = jnp.concatenate([kp_ref[0], kc_ref[0]], axis=0)
    vcat = jnp.concatenate([vp_ref[0], vc_ref[0]], axis=0)
    a = lax.broadcasted_iota(I32, (tq, 2 * tq), 0)
    c = lax.broadcasted_iota(I32, (tq, 2 * tq), 1)
    ok = (c >= a) & (c <= a + tq) & (c >= lo)
    lane_kv = _lane_iota((2 * tq, LANES))
    lane_o = _lane_iota((tq, LANES))
    m_all = jnp.zeros((tq, LANES), F32)
    den_all = jnp.ones((tq, LANES), F32)
    for p in range(H_C // 2):
        qp = q[:, p * LANES:(p + 1) * LANES]
        kp_ = kcat[:, p * LANES:(p + 1) * LANES]
        vp_ = vcat[:, p * LANES:(p + 1) * LANES]
        num_pair = jnp.zeros((tq, LANES), F32)
        for hh in range(2):
            hm = (lane_kv // HEAD_DIM) == hh
            s = _dot_nt(qp, jnp.where(hm, kp_, jnp.zeros_like(kp_)))
            s = jnp.where(ok, s, NEG)
            m = jnp.max(s, axis=-1, keepdims=True)
            e = jnp.exp(s - m)
            den = jnp.sum(e, axis=-1, keepdims=True)
            num_pair = num_pair + _dot(e, jnp.where(hm, vp_, jnp.zeros_like(vp_)))
            m_all = jnp.where(lane_o == 2 * p + hh, m, m_all)
            den_all = jnp.where(lane_o == 2 * p + hh, den, den_all)
        num_ref[0, :, p * LANES:(p + 1) * LANES] = num_pair
    m_ref[0] = m_all
    den_ref[0] = den_all


def _dsa_attention(q, k, v, dil):
    b, s, _ = q.shape
    tq = PAGE
    bpc = (s // dil) // tq
    cur = pl.BlockSpec((1, tq, HC_COLS), lambda bb, n: (bb, n, 0))
    prev = pl.BlockSpec((1, tq, HC_COLS), lambda bb, n: (bb, jnp.maximum(n - 1, 0), 0))
    stat = pl.BlockSpec((1, tq, LANES), lambda bb, n: (bb, n, 0))
    return pl.pallas_call(
        functools.partial(_dsa_kernel, bpc),
        grid=(b, s // tq),
        in_specs=[cur, cur, prev, cur, prev],
        out_specs=[cur, stat, stat],
        out_shape=[jax.ShapeDtypeStruct((b, s, HC_COLS), F32), jax.ShapeDtypeStruct((b, s, LANES), F32),
                   jax.ShapeDtypeStruct((b, s, LANES), F32)],
        compiler_params=_cparams(("arbitrary", "arbitrary")),
        name="l1_dsa_attention",
    )(q, k, k, v, v)


def _l1_out_kernel(tm, res_ref, gate1_ref, g2_ref, shift2_ref, scale2_ref, wo_ref, wr_ref, br_ref, eh_ref,
                   *rest):
    parts = rest[:3 * N_GROUPS]
    h1_ref, xm_ref, gate_ref, eidx_ref = rest[3 * N_GROUPS:3 * N_GROUPS + 4]
    scr = rest[3 * N_GROUPS + 4:]
    vals = []
    for g in range(N_GROUPS):
        d = C_DILATIONS[g]
        for t in range(3):
            ref, sc = parts[3 * g + t], scr[3 * g + t]
            if d == 1:
                vals.append(ref[0, 0])
            else:
                nch = sc.shape[0]
                for r in range(d):
                    blk = ref[0, r]
                    for cc in range(nch):
                        sc[cc, pl.ds(r, tm // d, stride=d), :] = blk[:, cc * LANES:(cc + 1) * LANES]
                vals.append(sc[0] if nch == 1 else jnp.concatenate([sc[cc] for cc in range(nch)], axis=1))
    nums, ms, dens = vals[0::3], vals[1::3], vals[2::3]
    mx = jnp.maximum(jnp.maximum(ms[0], ms[1]), ms[2])
    ws = [jnp.exp(mm - mx) for mm in ms]
    dsum = ws[0] * dens[0] + ws[1] * dens[1] + ws[2] * dens[2]
    o = jnp.zeros(nums[0].shape, F32)
    for g in range(N_GROUPS):
        o = o + _dot3(ws[g] / dsum, eh_ref[...]) * nums[g]
    h1 = res_ref[...] + gate1_ref[0] * _dot(o, wo_ref[...])
    h1_ref[...] = h1
    _router_epilogue(h1, g2_ref[...], shift2_ref[0], scale2_ref[0], wr_ref, br_ref, xm_ref, gate_ref, eidx_ref)


def _head_expand_mat():
    eh = np.zeros((LANES, HC_COLS), np.float32)
    for h in range(H_C):
        eh[h, h * HEAD_DIM:(h + 1) * HEAD_DIM] = 1.0
    return jnp.asarray(eh, BF16)


def _l1_out(parts, w_out, res, gate1, g2, shift2, scale2, wr, br, batch, seq):
    n = res.shape[0]
    tm = 256
    tps = seq // tm
    wo = w_out.astype(BF16)
    eh = _head_expand_mat()
    ins = [res, gate1, g2, shift2, scale2, wo, wr, br, eh]
    in_specs = [pl.BlockSpec((tm, D_MODEL), lambda i: (i, 0)), _mod_spec(gate1, tps), _const_spec(g2),
                _mod_spec(shift2, tps), _mod_spec(scale2, tps), _const_spec(wo), _const_spec(wr),
                _const_spec(br), _const_spec(eh)]
    scratch = []
    for gi in range(N_GROUPS):
        d = C_DILATIONS[gi]
        for t, arr in enumerate(parts[gi]):
            w = arr.shape[-1]
            ins.append(arr.reshape(batch, d, seq // d, w))
            in_specs.append(pl.BlockSpec((1, d, tm // d, w), lambda i: (i // tps, 0, i % tps, 0)))
            scratch.append(pltpu.VMEM((w // LANES, tm, LANES), F32))
    out_specs, out_shapes = _router_out_specs(n, tm)
    return pl.pallas_call(
        functools.partial(_l1_out_kernel, tm),
        grid=(n // tm,),
        in_specs=in_specs,
        out_specs=out_specs,
        out_shape=out_shapes,
        scratch_shapes=scratch,
        compiler_params=_cparams(("arbitrary",)),
        name="l1_out_router",
    )(*ins)


def _l1_sample_kernel(x_ref, shift_ref, scale_ref, g_ref, w_ref, sf_ref, gq_ref, gk_ref, cos_ref, sin_ref,
                      q_ref, k_ref, v_ref):
    xm = _modulate(x_ref[...], g_ref[...], scale_ref[0], shift_ref[0])
    u = _dot(xm, w_ref[...])
    cos = _tile_lanes(cos_ref[...], HC_COLS // LANES)
    sin = _tile_lanes(sin_ref[...], HC_COLS // LANES)
    for g in range(N_GROUPS):
        qn, kn, v = _l1_qkv(u, g, sf_ref[...], gq_ref[...], gk_ref[...], cos, sin)
        q_ref[:, g * HC_COLS:(g + 1) * HC_COLS] = qn
        k_ref[:, g * HC_COLS:(g + 1) * HC_COLS] = kn
        v_ref[:, g * HC_COLS:(g + 1) * HC_COLS] = v


def _l1_sample_proj(x, shift, scale, g, w_in, q_norm, k_norm, past):
    n = x.shape[0]
    w = w_in.astype(BF16)
    li = np.arange(LANES)
    sf = jnp.asarray((li[:, None] // HEAD_DIM) == (li[None, :] // HEAD_DIM), BF16)
    gq = (jnp.tile(q_norm, H_C) * HD_SCALE)[None]
    gk = jnp.tile(k_norm, H_C)[None]
    cos, sin = _l1_tables(np.full((1,), past))
    ins = [x, shift, scale, g, w, sf, gq, gk, cos, sin]
    wd = N_GROUPS * HC_COLS
    return pl.pallas_call(
        _l1_sample_kernel,
        grid=(1,),
        in_specs=[_const_spec(a) for a in ins],
        out_specs=[pl.BlockSpec((n, wd), lambda i: (0, 0))] * 3,
        out_shape=[jax.ShapeDtypeStruct((n, wd), F32)] * 3,
        compiler_params=_cparams(("arbitrary",)),
        name="l1_sample_proj",
    )(*ins)


DECODE_HEADS_PER_STEP = 4


def _dsa_decode_kernel(q_ref, kcol_ref, vcol_ref, krow_ref, vrow_ref, *rest):
    st = rest[:N_GROUPS]
    o_ref = rest[N_GROUPS]
    new = rest[N_GROUPS + 1:]
    for hh in range(DECODE_HEADS_PER_STEP):
        nums, ms, dens = [], [], []
        for g in range(N_GROUPS):
            w, d = C_WINDOWS[g], C_DILATIONS[g]
            kt = st[g][0, 0, hh]
            vt = st[g][0, 1, hh]
            q = q_ref[0, g, hh]
            s = _dot(jnp.broadcast_to(q, (SUBLANES, HEAD_DIM)), kt)[:1]
            lane = _lane_iota(s.shape)
            s = jnp.where(lane % d == 0, s, NEG)
            s_new = jnp.sum(q * krow_ref[0, g, hh], axis=-1, keepdims=True)
            m = jnp.maximum(jnp.max(s, axis=-1, keepdims=True), s_new)
            e = jnp.exp(s - m)
            e_new = jnp.exp(s_new - m)
            dens.append(jnp.sum(e, axis=-1, keepdims=True) + e_new)
            nums.append(_dot_nt(jnp.broadcast_to(e, (SUBLANES, w)), vt)[:1] + e_new * vrow_ref[0, g, hh])
            ms.append(m)
            lane2 = _lane_iota(kt.shape)
            new[g][0, 0, hh] = jnp.where(lane2 == w - 1, kcol_ref[0, g, hh], pltpu.roll(kt, w - 1, axis=1))
            new[g][0, 1, hh] = jnp.where(lane2 == w - 1, vcol_ref[0, g, hh], pltpu.roll(vt, w - 1, axis=1))
        mx = jnp.maximum(jnp.maximum(ms[0], ms[1]), ms[2])
        ws = [jnp.exp(mm - mx) for mm in ms]
        num = ws[0] * nums[0] + ws[1] * nums[1] + ws[2] * nums[2]
        den = ws[0] * dens[0] + ws[1] * dens[1] + ws[2] * dens[2]
        o_ref[0, hh] = num / den


def _dsa_decode(q, k, v, states):
    bd = q.shape[0]
    hps = DECODE_HEADS_PER_STEP
    q5 = q.reshape(bd, N_GROUPS, H_C, 1, HEAD_DIM)
    krow = k.reshape(bd, N_GROUPS, H_C, 1, HEAD_DIM)
    vrow = v.reshape(bd, N_GROUPS, H_C, 1, HEAD_DIM)
    kcol = k.reshape(bd, N_GROUPS, H_C, HEAD_DIM, 1)
    vcol = v.reshape(bd, N_GROUPS, H_C, HEAD_DIM, 1)
    row_spec = pl.BlockSpec((1, N_GROUPS, hps, 1, HEAD_DIM), lambda b, h: (b, 0, h, 0, 0))
    col_spec = pl.BlockSpec((1, N_GROUPS, hps, HEAD_DIM, 1), lambda b, h: (b, 0, h, 0, 0))
    st_specs = [pl.BlockSpec((1, 2, hps, HEAD_DIM, w), lambda b, h: (b, 0, h, 0, 0)) for w in C_WINDOWS]
    outs = pl.pallas_call(
        _dsa_decode_kernel,
        grid=(bd, H_C // hps),
        in_specs=[row_spec, col_spec, col_spec, row_spec, row_spec] + st_specs,
        out_specs=[pl.BlockSpec((1, hps, 1, HEAD_DIM), lambda b, h: (b, h, 0, 0))] + st_specs,
        out_shape=[jax.ShapeDtypeStruct((bd, H_C, 1, HEAD_DIM), F32)]
        + [jax.ShapeDtypeStruct(s.shape, F32) for s in states],
        compiler_params=_cparams(("arbitrary", "arbitrary")),
        name="l1_dsa_decode",
    )(q5, kcol, vcol, krow, vrow, *states)
    return outs[0].reshape(bd, HC_COLS), outs[1:]


def _out_router_plain(a, w_out, res, gate1, g2, shift2, scale2, wr, br, rows_per_group):
    return _out_router(a, w_out, res, gate1, g2, shift2, scale2, wr, br, rows_per_group)


def kernel(x_prompt, x_sample, cache_mla_ckv, cache_mla_krope, cache_fox_kv, cache_fox_logf, state_c1_kv, state_c2_kv, state_c3_kv, page_table, c_prompt, c_sample, ada_w, ada_b, norm1_g, norm2_g, ab_w_in, mla_qa_norm, mla_kv_norm, mla_w_qb, mla_w_kvb, mla_q_norm, mla_k_norm, fox_q_norm, fox_k_norm, fox_f_bias, ab_w_out, c_w_in, c_q_norm, c_k_norm, c_w_out, router_w, router_b, moe_w1, moe_b1, moe_w2, moe_b2):
    b, s, d = x_prompt.shape
    bd = x_sample.shape[0]
    assert x_sample.shape[1] == 1 and ada_w.shape[0] == 2
    n_pages = page_table.shape[1]
    past = n_pages * PAGE
    assert past >= max(C_WINDOWS) and n_pages % PAGES_PER_STEP == 0
    n = b * s

    pad = (-(b + bd)) % SUBLANES
    c_all = jnp.concatenate([c_prompt, c_sample, jnp.zeros((pad, d), F32)], axis=0)
    mod = _adaln(c_all, ada_w, ada_b)

    def mods(layer):
        mp = [mod[layer, :b, i * d:(i + 1) * d][:, None, :] for i in range(6)]
        ms = [mod[layer, b:b + bd, i * d:(i + 1) * d][None] for i in range(6)]
        return mp, ms

    hp = x_prompt.reshape(n, d)
    hs = x_sample.reshape(bd, d)
    moe_w1 = moe_w1.astype(BF16)
    moe_w2 = moe_w2.astype(BF16)

    mp, ms = mods(0)
    g1, g2 = norm1_g[0][None], norm2_g[0][None]
    wr, br = _router_consts(router_w[0], router_b[0])
    consts = _l0_consts(ab_w_in[0], mla_qa_norm[0], mla_kv_norm[0], mla_w_qb[0], mla_w_kvb[0], mla_q_norm[0],
                        mla_k_norm[0], fox_q_norm[0], fox_k_norm[0], fox_f_bias[0])
    q, k, v, p_ckv, p_kr, p_fkv, p_lf = _l0_prompt_proj(hp, mp[0], mp[1], g1, consts, s)
    o = _flash_attention(q.reshape(b, s, -1), k.reshape(b, s, -1), v.reshape(b, s, -1))
    h1, xm, gates, eidx = _out_router(o.reshape(n, -1), ab_w_out[0], hp, mp[2], g2, mp[3], mp[4], wr, br, s)
    hp = _moe(h1, xm, gates, eidx, mp[5], s, 0, moe_w1, moe_b1, moe_w2, moe_b2)

    q_s, s_ckv, s_kr, fq_s, s_fkv, s_lf = _l0_sample_proj(hs, ms[0], ms[1], g1, consts, past)
    npool = cache_mla_ckv.shape[1]
    krt_cache = jnp.transpose(cache_mla_krope[0], (0, 2, 1))
    kvt_cache = jnp.transpose(cache_fox_kv[0], (0, 2, 3, 4, 1)).reshape(npool, 2 * KV_B * HEAD_DIM, PAGE)
    lft_cache = jnp.transpose(cache_fox_logf[0], (0, 2, 1))
    oa = _mla_decode(page_table, q_s, s_ckv, s_kr, cache_mla_ckv[0], krt_cache, mla_w_kvb[0], mla_k_norm[0])
    ob = _fox_decode(page_table, fq_s, s_fkv, s_lf, kvt_cache, lft_cache)
    o_s = jnp.concatenate([oa, ob], axis=1).astype(BF16)
    h1, xm, gates, eidx = _out_router(o_s, ab_w_out[0], hs, ms[2], g2, ms[3], ms[4], wr, br, bd)
    hs = _moe(h1, xm, gates, eidx, ms[5], bd, 0, moe_w1, moe_b1, moe_w2, moe_b2)

    mp, ms = mods(1)
    g1, g2 = norm1_g[1][None], norm2_g[1][None]
    wr, br = _router_consts(router_w[1], router_b[1])
    outs = _l1_prompt_proj(hp, mp[0], mp[1], g1, c_w_in[0], c_q_norm[0], c_k_norm[0], b, s)
    kvlast = outs[3 * N_GROUPS]
    parts = []
    for gi in range(N_GROUPS):
        qg, kg, vg = (a.reshape(b, s, HC_COLS) for a in outs[3 * gi:3 * gi + 3])
        parts.append(_dsa_attention(qg, kg, vg, C_DILATIONS[gi]))
    h1, xm, gates, eidx = _l1_out(parts, c_w_out[0], hp, mp[2], g2, mp[3], mp[4], wr, br, b, s)
    hp = _moe(h1, xm, gates, eidx, mp[5], s, 1, moe_w1, moe_b1, moe_w2, moe_b2)

    q1, k1, v1 = _l1_sample_proj(hs, ms[0], ms[1], g1, c_w_in[0], c_q_norm[0], c_k_norm[0], past)
    states = [jnp.transpose(st[0], (0, 2, 3, 4, 1)) for st in (state_c1_kv, state_c2_kv, state_c3_kv)]
    o1, new_states = _dsa_decode(q1, k1, v1, states)
    h1, xm, gates, eidx = _out_router(o1.astype(BF16), c_w_out[0], hs, ms[2], g2, ms[3], ms[4], wr, br, bd)
    hs = _moe(h1, xm, gates, eidx, ms[5], bd, 1, moe_w1, moe_b1, moe_w2, moe_b2)

    wmax = max(C_WINDOWS)
    p_c = []
    for gi, w in enumerate(C_WINDOWS):
        blk = kvlast[:, wmax - w:, gi * 2 * HC_COLS:(gi + 1) * 2 * HC_COLS]
        p_c.append(blk.reshape(1, b, w, 2, H_C, HEAD_DIM))
    s_c = [jnp.transpose(ns, (0, 4, 1, 2, 3))[None] for ns in new_states]
    return (hp.reshape(b, s, d), hs.reshape(bd, 1, d),
            p_ckv.reshape(1, b, s, KV_LORA), p_kr.reshape(1, b, s, ROPE_D),
            p_fkv.reshape(1, b, s, 2, KV_B, HEAD_DIM), p_lf.reshape(1, b, s, H_B),
            p_c[0], p_c[1], p_c[2],
            s_ckv.reshape(1, bd, 1, KV_LORA), s_kr[:, :ROPE_D].reshape(1, bd, 1, ROPE_D),
            s_fkv.reshape(1, bd, 1, 2, KV_B, HEAD_DIM), s_lf[:, :H_B].reshape(1, bd, 1, H_B),
            s_c[0], s_c[1], s_c[2])
```

```python
import functools

import numpy as np
import jax
import jax.numpy as jnp
from jax import lax
from jax.experimental import pallas as pl
from jax.experimental.pallas import tpu as pltpu

F32 = jnp.float32
BF16 = jnp.bfloat16
I32 = jnp.int32

D_MODEL = 1024
PAGE = 128
HEAD_DIM = 64
H_A = 8
Q_LORA = 384
KV_LORA = 256
NOPE = 64
ROPE_D = 32
V_DIM = 64
H_B = 8
KV_B = 4
H_C = 8
C_WINDOWS = (128, 512, 2048)
C_DILATIONS = (1, 4, 16)
N_EXPERTS = 32
TOP_K = 4
D_FF = 1024
SWIGLU_ALPHA = 1.702
SWIGLU_LIMIT = 7.0
ROPE_THETA = 10000.0
NORM_EPS = 1e-6
NEG = -1e30
MLA_SCALE = (NOPE + ROPE_D) ** -0.5
HD_SCALE = HEAD_DIM ** -0.5

LANES = 128
SUBLANES = 8
VMEM_LIMIT = 56 * 1024 * 1024
ROW_TILE_CHUNKS = D_MODEL // LANES
FLASH_TILE = 1024
MOE_ROWS = 512
DMA_UNROLL = 8


def _cparams(sem):
    return pltpu.CompilerParams(dimension_semantics=sem, vmem_limit_bytes=VMEM_LIMIT)


def _dot(a, b):
    return jnp.dot(a.astype(BF16), b.astype(BF16), preferred_element_type=F32)


def _dot_nt(a, b):
    return lax.dot_general(a.astype(BF16), b.astype(BF16), (((1,), (1,)), ((), ())),
                           preferred_element_type=F32)


def _split3(x):
    hi = x.astype(BF16)
    r = x - hi.astype(F32)
    mid = r.astype(BF16)
    lo = (r - mid.astype(F32)).astype(BF16)
    return hi, mid, lo


def _dot3(a, b):
    hi, mid, lo = _split3(a)
    return (jnp.dot(hi, b, preferred_element_type=F32) + jnp.dot(mid, b, preferred_element_type=F32)
            + jnp.dot(lo, b, preferred_element_type=F32))


def _modulate(x, g, scale, shift):
    ms = jnp.mean(x * x, axis=-1, keepdims=True)
    return (x * lax.rsqrt(ms + NORM_EPS) * g) * (1.0 + scale) + shift


def _group_rms(x, s_blk, inv_cnt):
    sq = (x * x).astype(BF16)
    parts = [jnp.dot(sq[:, c * LANES:(c + 1) * LANES], s_blk, preferred_element_type=F32)
             for c in range(x.shape[1] // LANES)]
    ssq = parts[0] if len(parts) == 1 else jnp.concatenate(parts, axis=1)
    return lax.rsqrt(ssq * inv_cnt + NORM_EPS)


def _tile_lanes(x, n):
    return jnp.concatenate([x] * n, axis=1)


def _lane_iota(shape):
    return lax.broadcasted_iota(I32, shape, len(shape) - 1)


def _adaln_kernel(c_ref, w_ref, b_ref, o_ref):
    c = c_ref[...]
    s = c * jax.nn.sigmoid(c)
    o_ref[0] = _dot(s, w_ref[0]) + b_ref[0]


def _adaln(c_all, ada_w, ada_b):
    depth, d, n6 = ada_w.shape
    r = c_all.shape[0]
    tn = 768
    return pl.pallas_call(
        _adaln_kernel,
        grid=(depth, n6 // tn),
        in_specs=[pl.BlockSpec((r, d), lambda l, j: (0, 0)),
                  pl.BlockSpec((1, d, tn), lambda l, j: (l, 0, j)),
                  pl.BlockSpec((1, 1, tn), lambda l, j: (l, 0, j))],
        out_specs=pl.BlockSpec((1, r, tn), lambda l, j: (l, 0, j)),
        out_shape=jax.ShapeDtypeStruct((depth, r, n6), F32),
        compiler_params=_cparams(("arbitrary", "arbitrary")),
        name="adaln",
    )(c_all, ada_w, ada_b.reshape(depth, 1, n6))


def _mod_spec(arr, tiles_per_group):
    g, r, d = arr.shape
    if g == 1:
        return pl.BlockSpec((1, r, d), lambda i: (0, 0, 0))
    return pl.BlockSpec((1, r, d), lambda i: (i // tiles_per_group, 0, 0))


def _const_spec(arr):
    nd = arr.ndim
    return pl.BlockSpec(arr.shape, lambda *_: (0,) * nd)


def _cos_sin(pos, half):
    inv_freq = (np.float32(ROPE_THETA) ** (-np.arange(half, dtype=np.float32) / np.float32(half))).astype(np.float32)
    ang = np.asarray(pos, np.float32)[:, None] * inv_freq[None, :]
    return np.cos(ang).astype(np.float32), np.sin(ang).astype(np.float32)


def _l0_consts(w_in, qa_norm, kv_norm, w_qb, w_kvb, q_norm, k_norm, fq_norm, fk_norm, f_bias):
    d = w_in.shape[0]
    z = lambda n: jnp.zeros((d, n), F32)
    cq, ckv = w_in[:, :Q_LORA], w_in[:, Q_LORA:Q_LORA + KV_LORA]
    o = Q_LORA + KV_LORA
    kr = w_in[:, o:o + ROPE_D]
    o += ROPE_D
    fq = w_in[:, o:o + H_B * HEAD_DIM]
    o += H_B * HEAD_DIM
    fk = w_in[:, o:o + KV_B * HEAD_DIM]
    o += KV_B * HEAD_DIM
    fv = w_in[:, o:o + KV_B * HEAD_DIM]
    o += KV_B * HEAD_DIM
    fl = w_in[:, o:o + H_B]
    hr = ROPE_D // 2
    krr = jnp.concatenate([-kr[:, hr:], kr[:, :hr]], axis=1)
    g1 = jnp.concatenate([kr, z(32), kr, z(32)], axis=1)
    g2 = jnp.concatenate([krr, z(32), krr, z(32)], axis=1)
    w_in2 = jnp.concatenate([cq, ckv, g1, g2, fq, fk, fv, fl, z(LANES - H_B)], axis=1).astype(BF16)

    wq = w_qb.reshape(Q_LORA, H_A, NOPE + ROPE_D)
    zq = lambda n: jnp.zeros((Q_LORA, H_A, n), F32)
    rope_c = wq[:, :, NOPE:]
    rope_r = jnp.concatenate([-rope_c[:, :, hr:], rope_c[:, :, :hr]], axis=2)
    wqa = jnp.concatenate([wq[:, :, :NOPE], rope_c, zq(32)], axis=2).reshape(Q_LORA, H_A * LANES)
    wqb = jnp.concatenate([zq(NOPE), rope_r, zq(32)], axis=2).reshape(Q_LORA, H_A * LANES)
    w_q2 = jnp.concatenate([wqa, wqb], axis=1).astype(BF16)

    wkv = w_kvb.reshape(KV_LORA, H_A, NOPE + V_DIM)
    wk = jnp.concatenate([wkv[:, :, :NOPE], jnp.zeros((KV_LORA, H_A, LANES - NOPE), F32)], axis=2)
    w_kv2 = jnp.concatenate([wk.reshape(KV_LORA, H_A * LANES),
                             wkv[:, :, NOPE:].reshape(KV_LORA, H_A * V_DIM)], axis=1).astype(BF16)

    z32 = jnp.zeros((32,), F32)
    qn_r = q_norm[NOPE:]
    ga = jnp.tile(jnp.concatenate([q_norm[:NOPE], qn_r, z32]), H_A) * MLA_SCALE
    gb = jnp.tile(jnp.concatenate([jnp.zeros((NOPE,), F32), qn_r[hr:], qn_r[:hr], z32]), H_A) * MLA_SCALE
    gk = jnp.tile(jnp.concatenate([k_norm[:NOPE], jnp.zeros((LANES - NOPE,), F32)]), H_A)
    kn_r = k_norm[NOPE:]
    kn_rr = jnp.concatenate([kn_r[hr:], kn_r[:hr]])
    gk1 = jnp.concatenate([kn_r, z32, kn_r, z32])
    gk2 = jnp.concatenate([kn_rr, z32, kn_rr, z32])
    vecs = dict(
        qa_norm=qa_norm[None], kv_norm=kv_norm[None], ga=ga[None], gb=gb[None], gk=gk[None],
        gk1=gk1[None], gk2=gk2[None],
        gfq=(jnp.tile(fq_norm, H_B) * HD_SCALE)[None], gfk=jnp.tile(fk_norm, KV_B)[None],
        fbias=jnp.concatenate([f_bias, jnp.zeros((LANES - H_B,), F32)])[None],
    )

    li = np.arange(LANES)
    sq = ((li[:, None] < 64) & (li[None, :] < 64)) | ((li[:, None] >= 64) & (li[:, None] < 96)
                                                       & (li[None, :] >= 64) & (li[None, :] < 96))
    sk = (li[:, None] < 64) & (li[None, :] < 64)
    sf = (li[:, None] // 64) == (li[None, :] // 64)
    cnt_q = np.where(li < 64, 1.0 / 64, np.where(li < 96, 1.0 / 32, 1.0)).astype(np.float32)
    mats = dict(
        s_q=jnp.asarray(sq, BF16), s_k=jnp.asarray(sk, BF16), s_f=jnp.asarray(sf, BF16),
        cnt_q=jnp.asarray(np.tile(cnt_q, H_A))[None],
    )
    return w_in2, w_q2, w_kv2, vecs, mats


def _l0_common(u, wq2_ref, wkv2_ref, p, cq_t, sq_t, ck_t, sk_t):
    cq = u[:, :Q_LORA]
    ckv = u[:, Q_LORA:640]
    g1 = u[:, 640:768]
    g2 = u[:, 768:896]
    fq = u[:, 896:1408]
    fk = u[:, 1408:1664]
    fv = u[:, 1664:1920]
    fl = u[:, 1920:2048]

    cq_n = cq * lax.rsqrt(jnp.mean(cq * cq, axis=-1, keepdims=True) + NORM_EPS) * p["qa_norm"][...]
    q2 = _dot(cq_n, wq2_ref[...])
    qa, qb = q2[:, :H_A * LANES], q2[:, H_A * LANES:]
    rq = _group_rms(qa, p["s_q"][...], p["cnt_q"][...])
    q_mla = rq * (qa * p["ga"][...] * _tile_lanes(cq_t, H_A) + qb * p["gb"][...] * _tile_lanes(sq_t, H_A))

    ckv_n = ckv * lax.rsqrt(jnp.mean(ckv * ckv, axis=-1, keepdims=True) + NORM_EPS) * p["kv_norm"][...]
    kv2 = _dot(ckv_n, wkv2_ref[...])
    kk, v_mla = kv2[:, :H_A * LANES], kv2[:, H_A * LANES:]
    rk = _group_rms(kk, p["s_k"][...], 1.0 / NOPE)
    k_nope = kk * rk * p["gk"][...]

    lane = _lane_iota(g1.shape)
    ss = jnp.sum(jnp.where(lane < ROPE_D, g1 * g1, 0.0), axis=-1, keepdims=True)
    r_kr = lax.rsqrt(ss * (1.0 / ROPE_D) + NORM_EPS)
    kr128 = r_kr * (g1 * p["gk1"][...] * ck_t + g2 * p["gk2"][...] * sk_t)

    rfq = _group_rms(fq, p["s_f"][...], 1.0 / HEAD_DIM)
    fq_n = fq * rfq * p["gfq"][...]
    rfk = _group_rms(fk, p["s_f"][...], 1.0 / HEAD_DIM)
    fk_n = fk * rfk * p["gfk"][...]
    xl = fl + p["fbias"][...]
    logf = jnp.minimum(xl, 0.0) - jnp.log(1.0 + jnp.exp(-jnp.abs(xl)))
    return q_mla, ckv_n, k_nope, v_mla, kr128, fq_n, fk_n, fv, logf


_L0_VEC_NAMES = ("qa_norm", "kv_norm", "ga", "gb", "gk", "gk1", "gk2", "gfq", "gfk", "fbias")
_L0_MAT_NAMES = ("s_q", "s_k", "s_f", "cnt_q")


def _l0_prompt_kernel(tiles_per_seq, x_ref, shift_ref, scale_ref, g_ref, win_ref, wq2_ref, wkv2_ref,
                      cq_ref, sq_ref, ck_ref, sk_ref, eq_ref, ek_ref, ev_ref, pcf_ref, ltri_ref, ones_ref,
                      *rest):
    nv, nm = len(_L0_VEC_NAMES), len(_L0_MAT_NAMES)
    p = dict(zip(_L0_VEC_NAMES + _L0_MAT_NAMES, rest[:nv + nm]))
    q_out, k_out, v_out, ckv_out, kr_out, fkv_out, lf_out, carry = rest[nv + nm:]

    @pl.when(pl.program_id(0) % tiles_per_seq == 0)
    def _():
        carry[...] = jnp.zeros_like(carry)

    xm = _modulate(x_ref[...], g_ref[...], scale_ref[0], shift_ref[0])
    u = _dot(xm, win_ref[...])
    q_mla, ckv_n, k_nope, v_mla, kr128, fq_n, fk_n, fv, logf = _l0_common(
        u, wq2_ref, wkv2_ref, p, cq_ref[...], sq_ref[...], ck_ref[...], sk_ref[...])

    lane = _lane_iota(kr128.shape)
    k_mla = k_nope + _tile_lanes(jnp.where(lane >= NOPE, kr128, 0.0), H_A)

    ltri = ltri_ref[...]
    cf = _dot3_left(ltri, logf) + carry[...]
    carry[...] = cf[cf.shape[0] - 1:, :]
    nh, nm_, nl = _split3(-cf)
    bias = jnp.dot(jnp.concatenate([nh, nm_, nl], axis=1), pcf_ref[...], preferred_element_type=F32)

    q_fox = _dot(fq_n, eq_ref[...]) + ones_ref[...]
    k_fox = _dot(fk_n, ek_ref[...]) + bias
    v_fox = _dot(fv, ev_ref[...])

    q_out[...] = jnp.concatenate([q_mla, q_fox], axis=1).astype(BF16)
    k_out[...] = jnp.concatenate([k_mla, k_fox], axis=1).astype(BF16)
    v_out[...] = jnp.concatenate([v_mla, v_fox], axis=1).astype(BF16)
    ckv_out[...] = ckv_n
    kr_out[...] = kr128[:, :ROPE_D]
    fkv_out[...] = jnp.concatenate([fk_n, fv], axis=1)
    lf_out[...] = logf[:, :H_B]


def _dot3_left(m01, x):
    hi, mid, lo = _split3(x)
    return (jnp.dot(m01, hi, preferred_element_type=F32) + jnp.dot(m01, mid, preferred_element_type=F32)
            + jnp.dot(m01, lo, preferred_element_type=F32))


def _fox_place_mats():
    eq = np.zeros((H_B * HEAD_DIM, H_B * LANES), np.float32)
    ek = np.zeros((KV_B * HEAD_DIM, H_B * LANES), np.float32)
    ev = np.zeros((KV_B * HEAD_DIM, KV_B * LANES), np.float32)
    pcf = np.zeros((3 * LANES, H_B * LANES), np.float32)
    ones = np.zeros((1, H_B * LANES), np.float32)
    dd = np.arange(HEAD_DIM)
    for h in range(H_B):
        eq[h * HEAD_DIM + dd, h * LANES + dd] = 1.0
        ek[(h // 2) * HEAD_DIM + dd, h * LANES + dd] = 1.0
        for s in range(3):
            pcf[s * LANES + h, h * LANES + HEAD_DIM + s] = 1.0
            ones[0, h * LANES + HEAD_DIM + s] = 1.0
    for kvh in range(KV_B):
        for g in range(2):
            ev[kvh * HEAD_DIM + dd, kvh * LANES + g * HEAD_DIM + dd] = 1.0
    return (jnp.asarray(eq, BF16), jnp.asarray(ek, BF16), jnp.asarray(ev, BF16), jnp.asarray(pcf, BF16),
            jnp.asarray(ones, F32))


def _l0_tables(pos):
    c0, s0 = _cos_sin(pos, ROPE_D // 2)
    c = np.concatenate([c0] * 2, axis=1)
    s = np.concatenate([s0] * 2, axis=1)
    n = c.shape[0]
    one = np.ones((n, NOPE), np.float32)
    z32 = np.zeros((n, 32), np.float32)
    z64 = np.zeros((n, NOPE), np.float32)
    cq = np.concatenate([one, c, z32], axis=1)
    sq = np.concatenate([z64, s, z32], axis=1)
    ck = np.concatenate([c, z32, c, z32], axis=1)
    sk = np.concatenate([s, z32, s, z32], axis=1)
    return tuple(jnp.asarray(t) for t in (cq, sq, ck, sk))


def _l0_prompt_proj(x, shift, scale, g, consts, seq):
    n, d = x.shape
    tm = 256
    tps = seq // tm
    w_in2, w_q2, w_kv2, vecs, mats = consts
    cq, sq, ck, sk = _l0_tables(np.arange(seq))
    eq, ek, ev, pcf, ones = _fox_place_mats()
    ltri = jnp.asarray(np.tril(np.ones((tm, tm), np.float32)), BF16)
    tab_spec = pl.BlockSpec((tm, LANES), lambda i: (i % tps, 0))
    row = lambda w: pl.BlockSpec((tm, w), lambda i: (i, 0))
    small = [vecs[k] for k in _L0_VEC_NAMES] + [mats[k] for k in _L0_MAT_NAMES]
    ins = [x, shift, scale, g, w_in2, w_q2, w_kv2, cq, sq, ck, sk, eq, ek, ev, pcf, ltri, ones] + small
    in_specs = ([row(d), _mod_spec(shift, tps), _mod_spec(scale, tps), _const_spec(g), _const_spec(w_in2),
                 _const_spec(w_q2), _const_spec(w_kv2), tab_spec, tab_spec, tab_spec, tab_spec,
                 _const_spec(eq), _const_spec(ek), _const_spec(ev), _const_spec(pcf), _const_spec(ltri),
                 _const_spec(ones)] + [_const_spec(a) for a in small])
    widths = (2 * H_A * LANES, 2 * H_A * LANES, H_A * V_DIM + KV_B * LANES, KV_LORA, ROPE_D,
              2 * KV_B * HEAD_DIM, H_B)
    dtypes = (BF16, BF16, BF16, F32, F32, F32, F32)
    return pl.pallas_call(
        functools.partial(_l0_prompt_kernel, tps),
        grid=(n // tm,),
        in_specs=in_specs,
        out_specs=[row(w) for w in widths],
        out_shape=[jax.ShapeDtypeStruct((n, w), dt) for w, dt in zip(widths, dtypes)],
        scratch_shapes=[pltpu.VMEM((1, LANES), F32)],
        compiler_params=_cparams(("arbitrary",)),
        name="l0_prompt_proj",
    )(*ins)


def _l0_sample_kernel(x_ref, shift_ref, scale_ref, g_ref, win_ref, wq2_ref, wkv2_ref,
                      cq_ref, sq_ref, ck_ref, sk_ref, *rest):
    nv, nm = len(_L0_VEC_NAMES), len(_L0_MAT_NAMES)
    p = dict(zip(_L0_VEC_NAMES + _L0_MAT_NAMES, rest[:nv + nm]))
    q_out, ckv_out, kr_out, fq_out, fkv_out, lf_out = rest[nv + nm:]
    xm = _modulate(x_ref[...], g_ref[...], scale_ref[0], shift_ref[0])
    u = _dot(xm, win_ref[...])
    q_mla, ckv_n, _, _, kr128, fq_n, fk_n, fv, logf = _l0_common(
        u, wq2_ref, wkv2_ref, p, cq_ref[...], sq_ref[...], ck_ref[...], sk_ref[...])
    q_out[...] = q_mla
    ckv_out[...] = ckv_n
    kr_out[...] = kr128
    fq_out[...] = fq_n
    fkv_out[...] = jnp.concatenate([fk_n, fv], axis=1)
    lf_out[...] = logf


def _l0_sample_proj(x, shift, scale, g, consts, past):
    n, d = x.shape
    w_in2, w_q2, w_kv2, vecs, mats = consts
    tabs = _l0_tables(np.full((1,), past))
    small = [vecs[k] for k in _L0_VEC_NAMES] + [mats[k] for k in _L0_MAT_NAMES]
    ins = [x, shift, scale, g, w_in2, w_q2, w_kv2, *tabs] + small
    widths = (H_A * LANES, KV_LORA, LANES, H_B * HEAD_DIM, 2 * KV_B * HEAD_DIM, LANES)
    return pl.pallas_call(
        _l0_sample_kernel,
        grid=(1,),
        in_specs=[_const_spec(a) for a in ins],
        out_specs=[pl.BlockSpec((n, w), lambda i: (0, 0)) for w in widths],
        out_shape=[jax.ShapeDtypeStruct((n, w), F32) for w in widths],
        compiler_params=_cparams(("arbitrary",)),
        name="l0_sample_proj",
    )(*ins)


def _flash_kernel(tq, qi_ref, kj_ref, q_ref, k_ref, v_ref, o_ref, m_scr, l_scr, acc_scr):
    t = pl.program_id(2)
    i, j = qi_ref[t], kj_ref[t]
    tk = tq

    @pl.when(j == 0)
    def _():
        m_scr[...] = jnp.full(m_scr.shape, NEG, F32)
        l_scr[...] = jnp.zeros_like(l_scr)
        acc_scr[...] = jnp.zeros_like(acc_scr)

    def step(masked):
        q = q_ref[0]
        k = k_ref[0]
        v = v_ref[0]
        lane = _lane_iota((tq, LANES))
        new_acc = []
        for h in range(2):
            s = _dot_nt(q[:, h * LANES:(h + 1) * LANES], k[:, h * LANES:(h + 1) * LANES])
            if masked:
                row = lax.broadcasted_iota(I32, (tq, tk), 0)
                col = lax.broadcasted_iota(I32, (tq, tk), 1)
                s = jnp.where(col <= row, s, NEG)
            m_prev = m_scr[h]
            m_new = jnp.maximum(m_prev, jnp.max(s, axis=-1, keepdims=True))
            alpha = jnp.exp(m_prev - m_new)
            pr = jnp.exp(s - m_new[:, :1])
            l_scr[h] = alpha * l_scr[h] + jnp.sum(pr, axis=-1, keepdims=True)
            m_scr[h] = m_new
            new_acc.append(alpha * acc_scr[...] + _dot(pr, v))
        acc_scr[...] = jnp.where(lane < V_DIM, new_acc[0], new_acc[1])

    @pl.when(j < i)
    def _():
        step(False)

    @pl.when(j == i)
    def _():
        step(True)
        lane = _lane_iota((tq, LANES))
        l = jnp.where(lane < V_DIM, l_scr[0], l_scr[1])
        o_ref[0] = (acc_scr[...] / l).astype(o_ref.dtype)


def _flash_attention(q, k, v):
    b, s, _ = q.shape
    tq = FLASH_TILE if s % FLASH_TILE == 0 else 512
    nq = s // tq
    npair = H_A // 2 + H_B // 2
    qi = np.concatenate([np.full(i + 1, i) for i in range(nq)]).astype(np.int32)
    kj = np.concatenate([np.arange(i + 1) for i in range(nq)]).astype(np.int32)
    gs = pltpu.PrefetchScalarGridSpec(
        num_scalar_prefetch=2,
        grid=(b, npair, qi.shape[0]),
        in_specs=[pl.BlockSpec((1, tq, 2 * LANES), lambda bb, p, t, qi_, kj_: (bb, qi_[t], p)),
                  pl.BlockSpec((1, tq, 2 * LANES), lambda bb, p, t, qi_, kj_: (bb, kj_[t], p)),
                  pl.BlockSpec((1, tq, LANES), lambda bb, p, t, qi_, kj_: (bb, kj_[t], p))],
        out_specs=pl.BlockSpec((1, tq, LANES), lambda bb, p, t, qi_, kj_: (bb, qi_[t], p)),
        scratch_shapes=[pltpu.VMEM((2, tq, LANES), F32), pltpu.VMEM((2, tq, LANES), F32),
                        pltpu.VMEM((tq, LANES), F32)],
    )
    return pl.pallas_call(
        functools.partial(_flash_kernel, tq),
        grid_spec=gs,
        out_shape=jax.ShapeDtypeStruct((b, s, npair * LANES), BF16),
        compiler_params=_cparams(("arbitrary",) * 3),
        name="l0_flash",
    )(jnp.asarray(qi), jnp.asarray(kj), q, k, v)


def _router_epilogue(h1, g2, shift2, scale2, wr_ref, br_ref, xm_ref, gate_ref, eidx_ref):
    tm = h1.shape[0]
    xm = _modulate(h1, g2, scale2, shift2)
    for c in range(ROW_TILE_CHUNKS):
        xm_ref[pl.ds(c, tm, stride=SUBLANES), :] = xm[:, c * LANES:(c + 1) * LANES]
    logits = jnp.dot(xm, wr_ref[...], preferred_element_type=F32, precision=lax.Precision.HIGHEST) + br_ref[...]
    lane = _lane_iota(logits.shape)
    x = logits
    vals, ev = [], jnp.zeros(logits.shape, I32)
    for kk in range(TOP_K):
        m = jnp.max(x, axis=-1, keepdims=True)
        idx = jnp.min(jnp.where(x == m, lane, LANES), axis=-1, keepdims=True)
        vals.append(m)
        ev = jnp.where(lane == kk, idx, ev)
        x = jnp.where(lane == idx, -3e38, x)
    es = [jnp.exp(vv - vals[0]) for vv in vals]
    tot = es[0] + es[1] + es[2] + es[3]
    gv = jnp.zeros(logits.shape, F32)
    for kk in range(TOP_K):
        gv = jnp.where(lane == kk, es[kk] / tot, gv)
    gate_ref[...] = gv
    eidx_ref[...] = ev


def _out_router_kernel(a_ref, wo_ref, res_ref, gate1_ref, g2_ref, shift2_ref, scale2_ref, wr_ref, br_ref,
                       h1_ref, xm_ref, gate_ref, eidx_ref):
    h1 = res_ref[...] + gate1_ref[0] * _dot(a_ref[...], wo_ref[...])
    h1_ref[...] = h1
    _router_epilogue(h1, g2_ref[...], shift2_ref[0], scale2_ref[0], wr_ref, br_ref, xm_ref, gate_ref, eidx_ref)


def _router_consts(router_w, router_b):
    d, e = router_w.shape
    wr = jnp.concatenate([router_w, jnp.zeros((d, LANES - e), F32)], axis=1)
    br = jnp.concatenate([router_b, jnp.full((LANES - e,), NEG, F32)])[None]
    return wr, br


def _router_out_specs(n, tm):
    specs = [pl.BlockSpec((tm, D_MODEL), lambda i: (i, 0)),
             pl.BlockSpec((tm * SUBLANES, LANES), lambda i: (i, 0)),
             pl.BlockSpec((tm, LANES), lambda i: (i, 0)),
             pl.BlockSpec((tm, LANES), lambda i: (i, 0))]
    shapes = [jax.ShapeDtypeStruct((n, D_MODEL), F32), jax.ShapeDtypeStruct((n * SUBLANES, LANES), F32),
              jax.ShapeDtypeStruct((n, LANES), F32), jax.ShapeDtypeStruct((n, LANES), I32)]
    return specs, shapes


def _out_router(a, w_out, res, gate1, g2, shift2, scale2, wr, br, rows_per_group):
    n, ka = a.shape
    tm = min(256, n)
    tpg = max(rows_per_group // tm, 1)
    wo = w_out.astype(BF16)
    out_specs, out_shapes = _router_out_specs(n, tm)
    return pl.pallas_call(
        _out_router_kernel,
        grid=(n // tm,),
        in_specs=[pl.BlockSpec((tm, ka), lambda i: (i, 0)), _const_spec(wo),
                  pl.BlockSpec((tm, D_MODEL), lambda i: (i, 0)), _mod_spec(gate1, tpg), _const_spec(g2),
                  _mod_spec(shift2, tpg), _mod_spec(scale2, tpg), _const_spec(wr), _const_spec(br)],
        out_specs=out_specs,
        out_shape=out_shapes,
        compiler_params=_cparams(("arbitrary",)),
        name="out_router",
    )(a, wo, res, gate1, g2, shift2, scale2, wr, br)


def _moe_plan(eidx, tb):
    n = eidx.shape[0]
    m = n * TOP_K
    flat_e = eidx[:, :TOP_K].reshape(m)
    onehot = (flat_e[:, None] == jnp.arange(N_EXPERTS, dtype=I32)[None, :]).astype(I32)
    csum = jnp.cumsum(onehot, axis=0)
    counts = csum[-1]
    padded = (counts + tb - 1) // tb * tb
    pend = jnp.cumsum(padded)
    pstart = pend - padded
    pos = jnp.sum(onehot * (csum - 1 + pstart[None, :]), axis=1)
    nblk = -(-m // tb) + N_EXPERTS
    first_row = jnp.arange(nblk, dtype=I32) * tb
    blk_e = jnp.minimum(jnp.sum((pend[None, :] <= first_row[:, None]).astype(I32), axis=1), N_EXPERTS - 1)
    row_tok = jnp.zeros((nblk * tb,), I32).at[pos].set(jnp.arange(m, dtype=I32) // TOP_K)
    return pos.astype(I32), blk_e, row_tok, nblk


def _gather_rows(idx_ref, src_hbm, dst, sem, nrow):
    def body(g, c):
        for u in range(DMA_UNROLL):
            r = g * DMA_UNROLL + u
            t = idx_ref[0, 0, r]
            pltpu.make_async_copy(src_hbm.at[pl.ds(t * SUBLANES, SUBLANES)],
                                  dst.at[pl.ds(r * SUBLANES, SUBLANES)], sem).start()
        return c
    lax.fori_loop(0, nrow // DMA_UNROLL, body, 0)


def _wait_rows(src_hbm, dst, sem):
    pltpu.make_async_copy(src_hbm.at[pl.ds(0, dst.shape[0])], dst, sem).wait()


def _expert_kernel(tb, blk_e_ref, tok0_ref, tokn_ref, x_hbm, w1_ref, b1_ref, w2_ref, b2_ref, y_ref, xg0, xg1,
                   w1b, w2b, sem):
    i, nb = pl.program_id(0), pl.num_programs(0)
    bufs = (xg0, xg1)

    @pl.when((i == 0) | (blk_e_ref[i] != blk_e_ref[jnp.maximum(i - 1, 0)]))
    def _():
        w1b[...] = w1_ref[0, 0].astype(BF16)
        w2b[...] = w2_ref[0, 0].astype(BF16)

    @pl.when(i == 0)
    def _():
        _gather_rows(tok0_ref, x_hbm, xg0, sem.at[0], tb)

    def run(slot):
        @pl.when(i + 1 < nb)
        def _():
            _gather_rows(tokn_ref, x_hbm, bufs[1 - slot], sem.at[1 - slot], tb)

        xg = bufs[slot]
        _wait_rows(x_hbm, xg, sem.at[slot])
        x = jnp.concatenate([xg[pl.ds(c, tb, stride=SUBLANES), :] for c in range(ROW_TILE_CHUNKS)], axis=1)
        hcat = _dot(x, w1b[...]) + b1_ref[0, 0]
        glu = jnp.minimum(hcat[:, :D_FF], SWIGLU_LIMIT)
        lin = jnp.clip(hcat[:, D_FF:], -SWIGLU_LIMIT, SWIGLU_LIMIT)
        act = glu * jax.nn.sigmoid(SWIGLU_ALPHA * glu) * (lin + 1.0)
        y = _dot(act, w2b[...]) + b2_ref[0, 0]
        for c in range(ROW_TILE_CHUNKS):
            y_ref[pl.ds(c, tb, stride=SUBLANES), :] = y[:, c * LANES:(c + 1) * LANES]

    @pl.when(i % 2 == 0)
    def _():
        run(0)

    @pl.when(i % 2 == 1)
    def _():
        run(1)


def _experts(xm_tiles, blk_e, row_tok, nblk, tb, layer, w1, b1, w2, b2):
    nxt = lambda i, be: (jnp.minimum(i + 1, nblk - 1), 0, 0)
    gs = pltpu.PrefetchScalarGridSpec(
        num_scalar_prefetch=1,
        grid=(nblk,),
        in_specs=[pl.BlockSpec((1, 1, tb), lambda i, be: (0, 0, 0), memory_space=pltpu.SMEM),
                  pl.BlockSpec((1, 1, tb), nxt, memory_space=pltpu.SMEM),
                  pl.BlockSpec(memory_space=pl.ANY),
                  pl.BlockSpec((1, 1, D_MODEL, 2 * D_FF), lambda i, be: (layer, be[i], 0, 0)),
                  pl.BlockSpec((1, 1, 1, 2 * D_FF), lambda i, be: (layer, be[i], 0, 0)),
                  pl.BlockSpec((1, 1, D_FF, D_MODEL), lambda i, be: (layer, be[i], 0, 0)),
                  pl.BlockSpec((1, 1, 1, D_MODEL), lambda i, be: (layer, be[i], 0, 0))],
        out_specs=pl.BlockSpec((tb * SUBLANES, LANES), lambda i, be: (i, 0)),
        scratch_shapes=[pltpu.VMEM((tb * SUBLANES, LANES), F32), pltpu.VMEM((tb * SUBLANES, LANES), F32),
                        pltpu.VMEM((D_MODEL, 2 * D_FF), BF16), pltpu.VMEM((D_FF, D_MODEL), BF16),
                        pltpu.SemaphoreType.DMA((2,))],
    )
    depth, ne = b1.shape[:2]
    tok3 = row_tok.reshape(nblk, 1, tb)
    return pl.pallas_call(
        functools.partial(_expert_kernel, tb),
        grid_spec=gs,
        out_shape=jax.ShapeDtypeStruct((nblk * tb * SUBLANES, LANES), F32),
        compiler_params=_cparams(("arbitrary",)),
        name="moe_experts",
    )(blk_e, tok3, tok3, xm_tiles, w1, b1.reshape(depth, ne, 1, 2 * D_FF), w2,
      b2.reshape(depth, ne, 1, D_MODEL))


def _combine_kernel(tc, pos0_ref, posn_ref, y_hbm, gates_ref, h1_ref, gate2_ref, o_ref, buf0, buf1, sem):
    nrow = TOP_K * tc
    i, nb = pl.program_id(0), pl.num_programs(0)
    bufs = (buf0, buf1)

    @pl.when(i == 0)
    def _():
        _gather_rows(pos0_ref, y_hbm, buf0, sem.at[0], nrow)

    def run(slot):
        @pl.when(i + 1 < nb)
        def _():
            _gather_rows(posn_ref, y_hbm, bufs[1 - slot], sem.at[1 - slot], nrow)

        buf = bufs[slot]
        _wait_rows(y_hbm, buf, sem.at[slot])
        gates = gates_ref[...]
        cols = []
        for c in range(ROW_TILE_CHUNKS):
            acc = jnp.zeros((tc, LANES), F32)
            for kk in range(TOP_K):
                yk = buf[pl.ds(kk * tc * SUBLANES + c, tc, stride=SUBLANES), :]
                acc = acc + yk * gates[:, kk:kk + 1]
            cols.append(acc)
        moe = jnp.concatenate(cols, axis=1)
        o_ref[...] = h1_ref[...] + gate2_ref[0] * moe

    @pl.when(i % 2 == 0)
    def _():
        run(0)

    @pl.when(i % 2 == 1)
    def _():
        run(1)


def _combine(pos, ys, gates, h1, gate2, rows_per_group):
    n = h1.shape[0]
    tc = min(256, n)
    nt = n // tc
    tpg = max(rows_per_group // tc, 1)
    pos_blk = pos.reshape(nt, tc, TOP_K).transpose(0, 2, 1).reshape(nt, 1, TOP_K * tc)
    nbuf = TOP_K * tc * SUBLANES
    return pl.pallas_call(
        functools.partial(_combine_kernel, tc),
        grid=(nt,),
        in_specs=[pl.BlockSpec((1, 1, TOP_K * tc), lambda i: (0, 0, 0), memory_space=pltpu.SMEM),
                  pl.BlockSpec((1, 1, TOP_K * tc), lambda i: (jnp.minimum(i + 1, nt - 1), 0, 0),
                               memory_space=pltpu.SMEM),
                  pl.BlockSpec(memory_space=pl.ANY),
                  pl.BlockSpec((tc, LANES), lambda i: (i, 0)),
                  pl.BlockSpec((tc, D_MODEL), lambda i: (i, 0)),
                  _mod_spec(gate2, tpg)],
        out_specs=pl.BlockSpec((tc, D_MODEL), lambda i: (i, 0)),
        out_shape=jax.ShapeDtypeStruct((n, D_MODEL), F32),
        scratch_shapes=[pltpu.VMEM((nbuf, LANES), F32), pltpu.VMEM((nbuf, LANES), F32),
                        pltpu.SemaphoreType.DMA((2,))],
        compiler_params=_cparams(("arbitrary",)),
        name="moe_combine",
    )(pos_blk, pos_blk, ys, gates, h1, gate2)


def _moe(h1, xm_tiles, gates, eidx, gate2, rows_per_group, layer, w1, b1, w2, b2):
    n = h1.shape[0]
    tb = MOE_ROWS if n * TOP_K >= 16 * MOE_ROWS else PAGE
    pos, blk_e, row_tok, nblk = _moe_plan(eidx, tb)
    ys = _experts(xm_tiles, blk_e, row_tok, nblk, tb, layer, w1, b1, w2, b2)
    return _combine(pos, ys, gates, h1, gate2, rows_per_group)


PAGES_PER_STEP = 32


def _softmax_step(s, m_scr, l_scr):
    m_prev = m_scr[...]
    m_new = jnp.maximum(m_prev, jnp.max(s, axis=-1, keepdims=True))
    alpha = jnp.exp(m_prev - m_new)
    pr = jnp.exp(s - m_new[:, :1])
    l_scr[...] = alpha * l_scr[...] + jnp.sum(pr, axis=-1, keepdims=True)
    m_scr[...] = m_new
    return alpha[:, :1], pr


def _mla_decode_kernel(npg, pt_ref, q_ref, qcol_ref, cnew_ref, krnew_ref, gk_ref, gkcol_ref, t64_ref, mask_ref,
                       wkbt_ref, wvb_ref, *rest):
    ckv_refs, kr_refs = rest[:npg], rest[npg:2 * npg]
    o_ref, m_scr, l_scr, acc_scr = rest[2 * npg:]
    del pt_ref
    c = pl.program_id(1)

    @pl.when(c == 0)
    def _():
        m_scr[...] = jnp.full(m_scr.shape, NEG, F32)
        l_scr[...] = jnp.zeros_like(l_scr)
        acc_scr[...] = jnp.zeros_like(acc_scr)

    q8 = q_ref[0]
    mask = mask_ref[...]
    qr = q8[:, NOPE:NOPE + ROPE_D]
    cc = jnp.concatenate([r[0] for r in ckv_refs], axis=0).astype(BF16)
    kt = _dot_nt(wkbt_ref[...], cc)
    qg = qcol_ref[0] * gkcol_ref[...]
    ntile = kt.shape[1] // LANES
    row8 = lax.broadcasted_iota(I32, (SUBLANES, kt.shape[1]), 0)
    ssq = jnp.zeros((SUBLANES, kt.shape[1]), F32)
    tt = jnp.zeros((SUBLANES, kt.shape[1]), F32)
    for h in range(H_A):
        blk = kt[h * NOPE:(h + 1) * NOPE, :]
        qh = _tile_lanes(qg[h * NOPE:(h + 1) * NOPE, :], ntile)
        ssq = jnp.where(row8 == h, jnp.sum(blk * blk, axis=0, keepdims=True), ssq)
        tt = jnp.where(row8 == h, jnp.sum(blk * qh, axis=0, keepdims=True), tt)
    krt = jnp.concatenate([r[0] for r in kr_refs], axis=1)
    s = tt * lax.rsqrt(ssq * (1.0 / NOPE) + NORM_EPS) + _dot(qr, krt)
    alpha, pr = _softmax_step(s, m_scr, l_scr)
    acc_scr[...] = alpha * acc_scr[...] + _dot(pr, cc)

    @pl.when(c == pl.num_programs(1) - 1)
    def _():
        cnew = cnew_ref[0]
        qmat = _dot(q8[:, :NOPE] * gk_ref[...], t64_ref[...]) * mask
        knew = _dot_nt(jnp.broadcast_to(cnew, (SUBLANES, KV_LORA)), wkbt_ref[...])
        tt_n = jnp.sum(qmat * knew, axis=-1, keepdims=True)
        ssq_n = jnp.sum(mask * knew * knew, axis=-1, keepdims=True)
        s_n = (tt_n * lax.rsqrt(ssq_n * (1.0 / NOPE) + NORM_EPS)
               + jnp.sum(qr * krnew_ref[0][:, :ROPE_D], axis=-1, keepdims=True))
        alpha_n, p_n = _softmax_step(s_n, m_scr, l_scr)
        acc = alpha_n * acc_scr[...] + p_n * cnew
        lat = acc / l_scr[...][:, :1]
        o8 = _dot(lat, wvb_ref[...]) * mask
        o_ref[0] = jnp.sum(o8, axis=0, keepdims=True)


def _mla_decode(page_table, q_s, ckv_s, kr_s, ckv_cache, krt_cache, w_kvb, k_norm):
    bd, n_pages = page_table.shape
    npg = PAGES_PER_STEP
    wkv = w_kvb.reshape(KV_LORA, H_A, NOPE + V_DIM)
    wkbt = wkv[:, :, :NOPE].reshape(KV_LORA, H_A * NOPE).T.astype(BF16)
    wvb = wkv[:, :, NOPE:].reshape(KV_LORA, H_A * V_DIM).astype(BF16)
    hh = np.arange(H_A)[:, None]
    mask = jnp.asarray((np.arange(H_A * NOPE)[None, :] // NOPE) == hh, F32)
    t64 = jnp.asarray(np.tile(np.eye(NOPE, dtype=np.float32), (1, H_A)), BF16)
    gk = k_norm[:NOPE][None]
    gkcol = jnp.broadcast_to(jnp.tile(k_norm[:NOPE], H_A)[:, None], (H_A * NOPE, LANES))
    q8 = q_s.reshape(bd, H_A, LANES)
    qcol = jnp.broadcast_to(q8[:, :, :NOPE].reshape(bd, H_A * NOPE, 1), (bd, H_A * NOPE, LANES))
    consts = [gk, gkcol, t64, mask, wkbt, wvb]

    def page_spec(shape, k):
        return pl.BlockSpec((1,) + shape, lambda b, c, pt: (pt[b, c * npg + k], 0, 0))

    gs = pltpu.PrefetchScalarGridSpec(
        num_scalar_prefetch=1,
        grid=(bd, n_pages // npg),
        in_specs=([pl.BlockSpec((1, H_A, LANES), lambda b, c, pt: (b, 0, 0)),
                   pl.BlockSpec((1, H_A * NOPE, LANES), lambda b, c, pt: (b, 0, 0)),
                   pl.BlockSpec((1, 1, KV_LORA), lambda b, c, pt: (b, 0, 0)),
                   pl.BlockSpec((1, 1, LANES), lambda b, c, pt: (b, 0, 0))]
                  + [pl.BlockSpec(a.shape, lambda b, c, pt: (0, 0)) for a in consts]
                  + [page_spec((PAGE, KV_LORA), k) for k in range(npg)]
                  + [page_spec((ROPE_D, PAGE), k) for k in range(npg)]),
        out_specs=pl.BlockSpec((1, 1, H_A * V_DIM), lambda b, c, pt: (b, 0, 0)),
        scratch_shapes=[pltpu.VMEM((SUBLANES, LANES), F32), pltpu.VMEM((SUBLANES, LANES), F32),
                        pltpu.VMEM((SUBLANES, KV_LORA), F32)],
    )
    out = pl.pallas_call(
        functools.partial(_mla_decode_kernel, npg),
        grid_spec=gs,
        out_shape=jax.ShapeDtypeStruct((bd, 1, H_A * V_DIM), F32),
        compiler_params=_cparams(("arbitrary", "arbitrary")),
        name="l0_mla_decode",
    )(page_table, q8, qcol, ckv_s.reshape(bd, 1, KV_LORA), kr_s.reshape(bd, 1, LANES), *consts,
      *([ckv_cache] * npg), *([krt_cache] * npg))
    return out.reshape(bd, H_A * V_DIM)


def _prefix_lanes(x):
    n = x.shape[1]
    lane = _lane_iota(x.shape)
    sh = 1
    while sh < n:
        x = x + jnp.where(lane >= sh, pltpu.roll(x, sh, axis=1), 0.0)
        sh *= 2
    return x


def _fox_decode_kernel(npg, pt_ref, q_ref, knew_ref, vnew_ref, lfnew_ref, t64_ref, mask_ref,
                       pe_ref, po_ref, *rest):
    kv_refs, lf_refs = rest[:npg], rest[npg:2 * npg]
    o_ref, m_scr, l_scr, acc_scr, carry = rest[2 * npg:]
    del pt_ref
    c = pl.program_id(1)
    nkv = KV_B * HEAD_DIM

    @pl.when(c == 0)
    def _():
        m_scr[...] = jnp.full(m_scr.shape, NEG, F32)
        l_scr[...] = jnp.zeros_like(l_scr)
        acc_scr[...] = jnp.zeros_like(acc_scr)
        carry[...] = jnp.zeros_like(carry)

    mask = mask_ref[...]
    qblk = _dot(q_ref[0], t64_ref[...]) * mask
    kt = jnp.concatenate([r[0, :nkv, :] for r in kv_refs], axis=1)
    vt = jnp.concatenate([r[0, nkv:, :] for r in kv_refs], axis=1)
    lft = jnp.concatenate([r[0] for r in lf_refs], axis=1)
    cf = _prefix_lanes(lft) + carry[...][:, :1]
    carry[...] = jnp.broadcast_to(cf[:, cf.shape[1] - 1:], carry.shape)
    s = _dot(qblk, kt) - cf
    alpha, pr = _softmax_step(s, m_scr, l_scr)
    acc_scr[...] = alpha * acc_scr[...] + _dot_nt(pr, vt)

    @pl.when(c == pl.num_programs(1) - 1)
    def _():
        cf_t = carry[...][:, :1] + lfnew_ref[0][:, :1]
        s_n = jnp.sum(qblk * knew_ref[0], axis=-1, keepdims=True) - cf_t
        alpha_n, p_n = _softmax_step(s_n, m_scr, l_scr)
        acc = (alpha_n * acc_scr[...] + p_n * vnew_ref[0]) / l_scr[...][:, :1] * mask
        row = lax.broadcasted_iota(I32, acc.shape, 0)
        even = jnp.where(row % 2 == 0, acc, 0.0)
        odd = jnp.where(row % 2 == 1, acc, 0.0)
        o8 = _dot(even, pe_ref[...]) + _dot(odd, po_ref[...])
        o_ref[0] = jnp.sum(o8, axis=0, keepdims=True)


def _fox_decode(page_table, fq_s, fkv_s, lf_s, kvt_cache, lft_cache):
    bd, n_pages = page_table.shape
    npg = PAGES_PER_STEP
    nkv = KV_B * HEAD_DIM
    pc = npg * PAGE
    hh = np.arange(H_B)[:, None]
    mask = jnp.asarray((np.arange(nkv)[None, :] // HEAD_DIM) == hh // 2, F32)
    t64 = jnp.asarray(np.tile(np.eye(HEAD_DIM, dtype=np.float32), (1, KV_B)), BF16)
    pe = np.zeros((nkv, H_B * HEAD_DIM), np.float32)
    po = np.zeros((nkv, H_B * HEAD_DIM), np.float32)
    dd = np.arange(HEAD_DIM)
    for j in range(KV_B):
        pe[j * HEAD_DIM + dd, (2 * j) * HEAD_DIM + dd] = 1.0
        po[j * HEAD_DIM + dd, (2 * j + 1) * HEAD_DIM + dd] = 1.0
    consts = [t64, mask, jnp.asarray(pe, BF16), jnp.asarray(po, BF16)]
    lfnew = jnp.broadcast_to(lf_s[:, :H_B, None], (bd, H_B, LANES))

    def page_spec(shape, k):
        return pl.BlockSpec((1,) + shape, lambda b, c, pt: (pt[b, c * npg + k], 0, 0))

    gs = pltpu.PrefetchScalarGridSpec(
        num_scalar_prefetch=1,
        grid=(bd, n_pages // npg),
        in_specs=([pl.BlockSpec((1, H_B, HEAD_DIM), lambda b, c, pt: (b, 0, 0)),
                   pl.BlockSpec((1, 1, nkv), lambda b, c, pt: (b, 0, 0)),
                   pl.BlockSpec((1, 1, nkv), lambda b, c, pt: (b, 0, 0)),
                   pl.BlockSpec((1, H_B, LANES), lambda b, c, pt: (b, 0, 0))]
                  + [pl.BlockSpec(a.shape, lambda b, c, pt: (0, 0)) for a in consts]
                  + [page_spec((2 * nkv, PAGE), k) for k in range(npg)]
                  + [page_spec((H_B, PAGE), k) for k in range(npg)]),
        out_specs=pl.BlockSpec((1, 1, H_B * HEAD_DIM), lambda b, c, pt: (b, 0, 0)),
        scratch_shapes=[pltpu.VMEM((SUBLANES, LANES), F32), pltpu.VMEM((SUBLANES, LANES), F32),
                        pltpu.VMEM((SUBLANES, nkv), F32), pltpu.VMEM((SUBLANES, LANES), F32)],
    )
    out = pl.pallas_call(
        functools.partial(_fox_decode_kernel, npg),
        grid_spec=gs,
        out_shape=jax.ShapeDtypeStruct((bd, 1, H_B * HEAD_DIM), F32),
        compiler_params=_cparams(("arbitrary", "arbitrary")),
        name="l0_fox_decode",
    )(page_table, fq_s.reshape(bd, H_B, HEAD_DIM), fkv_s[:, :nkv].reshape(bd, 1, nkv),
      fkv_s[:, nkv:].reshape(bd, 1, nkv), lfnew, *consts, *([kvt_cache] * npg), *([lft_cache] * npg))
    return out.reshape(bd, H_B * HEAD_DIM)


N_GROUPS = len(C_WINDOWS)
GROUP_COLS = 3 * H_C * HEAD_DIM
HC_COLS = H_C * HEAD_DIM


def _rot_half64(x):
    n = x.shape[1]
    lane = _lane_iota(x.shape)
    fwd = pltpu.roll(x, n - HEAD_DIM // 2, axis=1)
    bwd = pltpu.roll(x, HEAD_DIM // 2, axis=1)
    return jnp.where(lane % HEAD_DIM < HEAD_DIM // 2, -fwd, bwd)


def _l1_qkv(u, g, s_f, gq, gk, cos, sin):
    base = g * GROUP_COLS
    q = u[:, base:base + HC_COLS]
    k = u[:, base + HC_COLS:base + 2 * HC_COLS]
    v = u[:, base + 2 * HC_COLS:base + 3 * HC_COLS]
    qn = q * _group_rms(q, s_f, 1.0 / HEAD_DIM) * gq
    kn = k * _group_rms(k, s_f, 1.0 / HEAD_DIM) * gk
    qn = qn * cos + _rot_half64(qn) * sin
    kn = kn * cos + _rot_half64(kn) * sin
    return qn, kn, v


def _l1_tables(pos):
    c, s = _cos_sin(pos, HEAD_DIM // 2)
    return jnp.asarray(np.concatenate([c] * 4, axis=1)), jnp.asarray(np.concatenate([s] * 4, axis=1))


def _l1_prompt_kernel(tm, x_ref, shift_ref, scale_ref, g_ref, w_ref, sf_ref, gq_ref, gk_ref, cos_ref, sin_ref,
                      *rest):
    outs, kvlast_ref, scr = rest[:3 * N_GROUPS], rest[3 * N_GROUPS], rest[3 * N_GROUPS + 1]
    xm = _modulate(x_ref[...], g_ref[...], scale_ref[0], shift_ref[0])
    u = _dot(xm, w_ref[...])
    cos = _tile_lanes(cos_ref[...], HC_COLS // LANES)
    sin = _tile_lanes(sin_ref[...], HC_COLS // LANES)
    for g in range(N_GROUPS):
        d = C_DILATIONS[g]
        qn, kn, v = _l1_qkv(u, g, sf_ref[...], gq_ref[...], gk_ref[...], cos, sin)
        kvlast_ref[0, :, g * 2 * HC_COLS:g * 2 * HC_COLS + HC_COLS] = kn
        kvlast_ref[0, :, g * 2 * HC_COLS + HC_COLS:(g + 1) * 2 * HC_COLS] = v
        for t, val in enumerate((qn, kn, v)):
            o_ref = outs[3 * g + t]
            if d == 1:
                o_ref[0, 0] = val.astype(BF16)
            else:
                for cc in range(HC_COLS // LANES):
                    scr[cc] = val[:, cc * LANES:(cc + 1) * LANES]
                for r in range(d):
                    o_ref[0, r] = jnp.concatenate(
                        [scr[cc, pl.ds(r, tm // d, stride=d), :] for cc in range(HC_COLS // LANES)],
                        axis=1).astype(BF16)


def _l1_prompt_proj(x, shift, scale, g, w_in, q_norm, k_norm, batch, seq):
    n, dm = x.shape
    tm = 256
    tps = seq // tm
    wmax = max(C_WINDOWS)
    assert seq >= wmax and wmax % tm == 0
    w = w_in.astype(BF16)
    li = np.arange(LANES)
    sf = jnp.asarray((li[:, None] // HEAD_DIM) == (li[None, :] // HEAD_DIM), BF16)
    gq = (jnp.tile(q_norm, H_C) * HD_SCALE)[None]
    gk = jnp.tile(k_norm, H_C)[None]
    cos, sin = _l1_tables(np.arange(seq))
    tab = pl.BlockSpec((tm, LANES), lambda i: (i % tps, 0))
    first_kept = tps - wmax // tm
    out_specs, out_shapes = [], []
    for gi in range(N_GROUPS):
        d = C_DILATIONS[gi]
        for _ in range(3):
            out_specs.append(pl.BlockSpec((1, d, tm // d, HC_COLS), lambda i: (i // tps, 0, i % tps, 0)))
            out_shapes.append(jax.ShapeDtypeStruct((batch, d, seq // d, HC_COLS), BF16))
    out_specs.append(pl.BlockSpec((1, tm, 2 * N_GROUPS * HC_COLS),
                                  lambda i: (i // tps, jnp.maximum(i % tps - first_kept, 0), 0)))
    out_shapes.append(jax.ShapeDtypeStruct((batch, wmax, 2 * N_GROUPS * HC_COLS), F32))
    ins = [x, shift, scale, g, w, sf, gq, gk, cos, sin]
    return pl.pallas_call(
        functools.partial(_l1_prompt_kernel, tm),
        grid=(n // tm,),
        in_specs=[pl.BlockSpec((tm, dm), lambda i: (i, 0)), _mod_spec(shift, tps), _mod_spec(scale, tps),
                  _const_spec(g), _const_spec(w), _const_spec(sf), _const_spec(gq), _const_spec(gk), tab, tab],
        out_specs=out_specs,
        out_shape=out_shapes,
        scratch_shapes=[pltpu.VMEM((HC_COLS // LANES, tm, LANES), F32)],
        compiler_params=_cparams(("arbitrary",)),
        name="l1_prompt_proj",
    )(*ins)


def _dsa_kernel(bpc, q_ref, kc_ref, kp_ref, vc_ref, vp_ref, num_ref, m_ref, den_ref):
    n = pl.program_id(1)
    tq = q_ref.shape[1]
    lo = jnp.where(n % bpc == 0, tq, 0)
    q = q_ref[0]
    kcat = jnp.concatenate([kp_ref[0], kc_ref[0]], axis=0)
    vcat = jnp.concatenate([vp_ref[0], vc_ref[0]], axis=0)
    a = lax.broadcasted_iota(I32, (tq, 2 * tq), 0)
    c = lax.broadcasted_iota(I32, (tq, 2 * tq), 1)
    ok = (c >= a) & (c <= a + tq) & (c >= lo)
    lane_kv = _lane_iota((2 * tq, LANES))
    lane_o = _lane_iota((tq, LANES))
    m_all = jnp.zeros((tq, LANES), F32)
    den_all = jnp.ones((tq, LANES), F32)
    for p in range(H_C // 2):
        qp = q[:, p * LANES:(p + 1) * LANES]
        kp_ = kcat[:, p * LANES:(p + 1) * LANES]
        vp_ = vcat[:, p * LANES:(p + 1) * LANES]
        num_pair = jnp.zeros((tq, LANES), F32)
        for hh in range(2):
            hm = (lane_kv // HEAD_DIM) == hh
            s = _dot_nt(qp, jnp.where(hm, kp_, jnp.zeros_like(kp_)))
            s = jnp.where(ok, s, NEG)
            m = jnp.max(s, axis=-1, keepdims=True)
            e = jnp.exp(s - m)
            den = jnp.sum(e, axis=-1, keepdims=True)
            num_pair = num_pair + _dot(e, jnp.where(hm, vp_, jnp.zeros_like(vp_)))
            m_all = jnp.where(lane_o == 2 * p + hh, m, m_all)
            den_all = jnp.where(lane_o == 2 * p + hh, den, den_all)
        num_ref[0, :, p * LANES:(p + 1) * LANES] = num_pair
    m_ref[0] = m_all
    den_ref[0] = den_all


def _dsa_attention(q, k, v, dil):
    b, s, _ = q.shape
    tq = PAGE
    bpc = (s // dil) // tq
    cur = pl.BlockSpec((1, tq, HC_COLS), lambda bb, n: (bb, n, 0))
    prev = pl.BlockSpec((1, tq, HC_COLS), lambda bb, n: (bb, jnp.maximum(n - 1, 0), 0))
    stat = pl.BlockSpec((1, tq, LANES), lambda bb, n: (bb, n, 0))
    return pl.pallas_call(
        functools.partial(_dsa_kernel, bpc),
        grid=(b, s // tq),
        in_specs=[cur, cur, prev, cur, prev],
        out_specs=[cur, stat, stat],
        out_shape=[jax.ShapeDtypeStruct((b, s, HC_COLS), F32), jax.ShapeDtypeStruct((b, s, LANES), F32),
                   jax.ShapeDtypeStruct((b, s, LANES), F32)],
        compiler_params=_cparams(("arbitrary", "arbitrary")),
        name="l1_dsa_attention",
    )(q, k, k, v, v)


def _l1_out_kernel(tm, res_ref, gate1_ref, g2_ref, shift2_ref, scale2_ref, wo_ref, wr_ref, br_ref, eh_ref,
                   *rest):
    parts = rest[:3 * N_GROUPS]
    h1_ref, xm_ref, gate_ref, eidx_ref = rest[3 * N_GROUPS:3 * N_GROUPS + 4]
    scr = rest[3 * N_GROUPS + 4:]
    vals = []
    for g in range(N_GROUPS):
        d = C_DILATIONS[g]
        for t in range(3):
            ref, sc = parts[3 * g + t], scr[3 * g + t]
            if d == 1:
                vals.append(ref[0, 0])
            else:
                nch = sc.shape[0]
                for r in range(d):
                    blk = ref[0, r]
                    for cc in range(nch):
                        sc[cc, pl.ds(r, tm // d, stride=d), :] = blk[:, cc * LANES:(cc + 1) * LANES]
                vals.append(sc[0] if nch == 1 else jnp.concatenate([sc[cc] for cc in range(nch)], axis=1))
    nums, ms, dens = vals[0::3], vals[1::3], vals[2::3]
    mx = jnp.maximum(jnp.maximum(ms[0], ms[1]), ms[2])
    ws = [jnp.exp(mm - mx) for mm in ms]
    dsum = ws[0] * dens[0] + ws[1] * dens[1] + ws[2] * dens[2]
    o = jnp.zeros(nums[0].shape, F32)
    for g in range(N_GROUPS):
        o = o + _dot3(ws[g] / dsum, eh_ref[...]) * nums[g]
    h1 = res_ref[...] + gate1_ref[0] * _dot(o, wo_ref[...])
    h1_ref[...] = h1
    _router_epilogue(h1, g2_ref[...], shift2_ref[0], scale2_ref[0], wr_ref, br_ref, xm_ref, gate_ref, eidx_ref)


def _head_expand_mat():
    eh = np.zeros((LANES, HC_COLS), np.float32)
    for h in range(H_C):
        eh[h, h * HEAD_DIM:(h + 1) * HEAD_DIM] = 1.0
    return jnp.asarray(eh, BF16)


def _l1_out(parts, w_out, res, gate1, g2, shift2, scale2, wr, br, batch, seq):
    n = res.shape[0]
    tm = 256
    tps = seq // tm
    wo = w_out.astype(BF16)
    eh = _head_expand_mat()
    ins = [res, gate1, g2, shift2, scale2, wo, wr, br, eh]
    in_specs = [pl.BlockSpec((tm, D_MODEL), lambda i: (i, 0)), _mod_spec(gate1, tps), _const_spec(g2),
                _mod_spec(shift2, tps), _mod_spec(scale2, tps), _const_spec(wo), _const_spec(wr),
                _const_spec(br), _const_spec(eh)]
    scratch = []
    for gi in range(N_GROUPS):
        d = C_DILATIONS[gi]
        for t, arr in enumerate(parts[gi]):
            w = arr.shape[-1]
            ins.append(arr.reshape(batch, d, seq // d, w))
            in_specs.append(pl.BlockSpec((1, d, tm // d, w), lambda i: (i // tps, 0, i % tps, 0)))
            scratch.append(pltpu.VMEM((w // LANES, tm, LANES), F32))
    out_specs, out_shapes = _router_out_specs(n, tm)
    return pl.pallas_call(
        functools.partial(_l1_out_kernel, tm),
        grid=(n // tm,),
        in_specs=in_specs,
        out_specs=out_specs,
        out_shape=out_shapes,
        scratch_shapes=scratch,
        compiler_params=_cparams(("arbitrary",)),
        name="l1_out_router",
    )(*ins)


def _l1_sample_kernel(x_ref, shift_ref, scale_ref, g_ref, w_ref, sf_ref, gq_ref, gk_ref, cos_ref, sin_ref,
                      q_ref, k_ref, v_ref):
    xm = _modulate(x_ref[...], g_ref[...], scale_ref[0], shift_ref[0])
    u = _dot(xm, w_ref[...])
    cos = _tile_lanes(cos_ref[...], HC_COLS // LANES)
    sin = _tile_lanes(sin_ref[...], HC_COLS // LANES)
    for g in range(N_GROUPS):
        qn, kn, v = _l1_qkv(u, g, sf_ref[...], gq_ref[...], gk_ref[...], cos, sin)
        q_ref[:, g * HC_COLS:(g + 1) * HC_COLS] = qn
        k_ref[:, g * HC_COLS:(g + 1) * HC_COLS] = kn
        v_ref[:, g * HC_COLS:(g + 1) * HC_COLS] = v


def _l1_sample_proj(x, shift, scale, g, w_in, q_norm, k_norm, past):
    n = x.shape[0]
    w = w_in.astype(BF16)
    li = np.arange(LANES)
    sf = jnp.asarray((li[:, None] // HEAD_DIM) == (li[None, :] // HEAD_DIM), BF16)
    gq = (jnp.tile(q_norm, H_C) * HD_SCALE)[None]
    gk = jnp.tile(k_norm, H_C)[None]
    cos, sin = _l1_tables(np.full((1,), past))
    ins = [x, shift, scale, g, w, sf, gq, gk, cos, sin]
    wd = N_GROUPS * HC_COLS
    return pl.pallas_call(
        _l1_sample_kernel,
        grid=(1,),
        in_specs=[_const_spec(a) for a in ins],
        out_specs=[pl.BlockSpec((n, wd), lambda i: (0, 0))] * 3,
        out_shape=[jax.ShapeDtypeStruct((n, wd), F32)] * 3,
        compiler_params=_cparams(("arbitrary",)),
        name="l1_sample_proj",
    )(*ins)


DECODE_HEADS_PER_STEP = 4


def _dsa_decode_kernel(q_ref, kcol_ref, vcol_ref, krow_ref, vrow_ref, *rest):
    st = rest[:N_GROUPS]
    o_ref = rest[N_GROUPS]
    new = rest[N_GROUPS + 1:]
    for hh in range(DECODE_HEADS_PER_STEP):
        nums, ms, dens = [], [], []
        for g in range(N_GROUPS):
            w, d = C_WINDOWS[g], C_DILATIONS[g]
            kt = st[g][0, 0, hh]
            vt = st[g][0, 1, hh]
            q = q_ref[0, g, hh]
            s = _dot(jnp.broadcast_to(q, (SUBLANES, HEAD_DIM)), kt)[:1]
            lane = _lane_iota(s.shape)
            s = jnp.where(lane % d == 0, s, NEG)
            s_new = jnp.sum(q * krow_ref[0, g, hh], axis=-1, keepdims=True)
            m = jnp.maximum(jnp.max(s, axis=-1, keepdims=True), s_new)
            e = jnp.exp(s - m)
            e_new = jnp.exp(s_new - m)
            dens.append(jnp.sum(e, axis=-1, keepdims=True) + e_new)
            nums.append(_dot_nt(jnp.broadcast_to(e, (SUBLANES, w)), vt)[:1] + e_new * vrow_ref[0, g, hh])
            ms.append(m)
            lane2 = _lane_iota(kt.shape)
            new[g][0, 0, hh] = jnp.where(lane2 == w - 1, kcol_ref[0, g, hh], pltpu.roll(kt, w - 1, axis=1))
            new[g][0, 1, hh] = jnp.where(lane2 == w - 1, vcol_ref[0, g, hh], pltpu.roll(vt, w - 1, axis=1))
        mx = jnp.maximum(jnp.maximum(ms[0], ms[1]), ms[2])
        ws = [jnp.exp(mm - mx) for mm in ms]
        num = ws[0] * nums[0] + ws[1] * nums[1] + ws[2] * nums[2]
        den = ws[0] * dens[0] + ws[1] * dens[1] + ws[2] * dens[2]
        o_ref[0, hh] = num / den


def _dsa_decode(q, k, v, states):
    bd = q.shape[0]
    hps = DECODE_HEADS_PER_STEP
    q5 = q.reshape(bd, N_GROUPS, H_C, 1, HEAD_DIM)
    krow = k.reshape(bd, N_GROUPS, H_C, 1, HEAD_DIM)
    vrow = v.reshape(bd, N_GROUPS, H_C, 1, HEAD_DIM)
    kcol = k.reshape(bd, N_GROUPS, H_C, HEAD_DIM, 1)
    vcol = v.reshape(bd, N_GROUPS, H_C, HEAD_DIM, 1)
    row_spec = pl.BlockSpec((1, N_GROUPS, hps, 1, HEAD_DIM), lambda b, h: (b, 0, h, 0, 0))
    col_spec = pl.BlockSpec((1, N_GROUPS, hps, HEAD_DIM, 1), lambda b, h: (b, 0, h, 0, 0))
    st_specs = [pl.BlockSpec((1, 2, hps, HEAD_DIM, w), lambda b, h: (b, 0, h, 0, 0)) for w in C_WINDOWS]
    outs = pl.pallas_call(
        _dsa_decode_kernel,
        grid=(bd, H_C // hps),
        in_specs=[row_spec, col_spec, col_spec, row_spec, row_spec] + st_specs,
        out_specs=[pl.BlockSpec((1, hps, 1, HEAD_DIM), lambda b, h: (b, h, 0, 0))] + st_specs,
        out_shape=[jax.ShapeDtypeStruct((bd, H_C, 1, HEAD_DIM), F32)]
        + [jax.ShapeDtypeStruct(s.shape, F32) for s in states],
        compiler_params=_cparams(("arbitrary", "arbitrary")),
        name="l1_dsa_decode",
    )(q5, kcol, vcol, krow, vrow, *states)
    return outs[0].reshape(bd, HC_COLS), outs[1:]


def _out_router_plain(a, w_out, res, gate1, g2, shift2, scale2, wr, br, rows_per_group):
    return _out_router(a, w_out, res, gate1, g2, shift2, scale2, wr, br, rows_per_group)


def kernel(x_prompt, x_sample, cache_mla_ckv, cache_mla_krope, cache_fox_kv, cache_fox_logf, state_c1_kv, state_c2_kv, state_c3_kv, page_table, c_prompt, c_sample, ada_w, ada_b, norm1_g, norm2_g, ab_w_in, mla_qa_norm, mla_kv_norm, mla_w_qb, mla_w_kvb, mla_q_norm, mla_k_norm, fox_q_norm, fox_k_norm, fox_f_bias, ab_w_out, c_w_in, c_q_norm, c_k_norm, c_w_out, router_w, router_b, moe_w1, moe_b1, moe_w2, moe_b2):
    b, s, d = x_prompt.shape
    bd = x_sample.shape[0]
    assert x_sample.shape[1] == 1 and ada_w.shape[0] == 2
    n_pages = page_table.shape[1]
    past = n_pages * PAGE
    assert past >= max(C_WINDOWS) and n_pages % PAGES_PER_STEP == 0
    n = b * s

    pad = (-(b + bd)) % SUBLANES
    c_all = jnp.concatenate([c_prompt, c_sample, jnp.zeros((pad, d), F32)], axis=0)
    mod = _adaln(c_all, ada_w, ada_b)

    def mods(layer):
        mp = [mod[layer, :b, i * d:(i + 1) * d][:, None, :] for i in range(6)]
        ms = [mod[layer, b:b + bd, i * d:(i + 1) * d][None] for i in range(6)]
        return mp, ms

    hp = x_prompt.reshape(n, d)
    hs = x_sample.reshape(bd, d)

    mp, ms = mods(0)
    g1, g2 = norm1_g[0][None], norm2_g[0][None]
    wr, br = _router_consts(router_w[0], router_b[0])
    consts = _l0_consts(ab_w_in[0], mla_qa_norm[0], mla_kv_norm[0], mla_w_qb[0], mla_w_kvb[0], mla_q_norm[0],
                        mla_k_norm[0], fox_q_norm[0], fox_k_norm[0], fox_f_bias[0])
    q, k, v, p_ckv, p_kr, p_fkv, p_lf = _l0_prompt_proj(hp, mp[0], mp[1], g1, consts, s)
    o = _flash_attention(q.reshape(b, s, -1), k.reshape(b, s, -1), v.reshape(b, s, -1))
    h1, xm, gates, eidx = _out_router(o.reshape(n, -1), ab_w_out[0], hp, mp[2], g2, mp[3], mp[4], wr, br, s)
    hp = _moe(h1, xm, gates, eidx, mp[5], s, 0, moe_w1, moe_b1, moe_w2, moe_b2)

    q_s, s_ckv, s_kr, fq_s, s_fkv, s_lf = _l0_sample_proj(hs, ms[0], ms[1], g1, consts, past)
    npool = cache_mla_ckv.shape[1]
    krt_cache = jnp.transpose(cache_mla_krope[0], (0, 2, 1))
    kvt_cache = jnp.transpose(cache_fox_kv[0], (0, 2, 3, 4, 1)).reshape(npool, 2 * KV_B * HEAD_DIM, PAGE)
    lft_cache = jnp.transpose(cache_fox_logf[0], (0, 2, 1))
    oa = _mla_decode(page_table, q_s, s_ckv, s_kr, cache_mla_ckv[0], krt_cache, mla_w_kvb[0], mla_k_norm[0])
    ob = _fox_decode(page_table, fq_s, s_fkv, s_lf, kvt_cache, lft_cache)
    o_s = jnp.concatenate([oa, ob], axis=1).astype(BF16)
    h1, xm, gates, eidx = _out_router(o_s, ab_w_out[0], hs, ms[2], g2, ms[3], ms[4], wr, br, bd)
    hs = _moe(h1, xm, gates, eidx, ms[5], bd, 0, moe_w1, moe_b1, moe_w2, moe_b2)

    mp, ms = mods(1)
    g1, g2 = norm1_g[1][None], norm2_g[1][None]
    wr, br = _router_consts(router_w[1], router_b[1])
    outs = _l1_prompt_proj(hp, mp[0], mp[1], g1, c_w_in[0], c_q_norm[0], c_k_norm[0], b, s)
    kvlast = outs[3 * N_GROUPS]
    parts = []
    for gi in range(N_GROUPS):
        qg, kg, vg = (a.reshape(b, s, HC_COLS) for a in outs[3 * gi:3 * gi + 3])
        parts.append(_dsa_attention(qg, kg, vg, C_DILATIONS[gi]))
    h1, xm, gates, eidx = _l1_out(parts, c_w_out[0], hp, mp[2], g2, mp[3], mp[4], wr, br, b, s)
    hp = _moe(h1, xm, gates, eidx, mp[5], s, 1, moe_w1, moe_b1, moe_w2, moe_b2)

    q1, k1, v1 = _l1_sample_proj(hs, ms[0], ms[1], g1, c_w_in[0], c_q_norm[0], c_k_norm[0], past)
    states = [jnp.transpose(st[0], (0, 2, 3, 4, 1)) for st in (state_c1_kv, state_c2_kv, state_c3_kv)]
    o1, new_states = _dsa_decode(q1, k1, v1, states)
    h1, xm, gates, eidx = _out_router(o1.astype(BF16), c_w_out[0], hs, ms[2], g2, ms[3], ms[4], wr, br, bd)
    hs = _moe(h1, xm, gates, eidx, ms[5], bd, 1, moe_w1, moe_b1, moe_w2, moe_b2)

    wmax = max(C_WINDOWS)
    p_c = []
    for gi, w in enumerate(C_WINDOWS):
        blk = kvlast[:, wmax - w:, gi * 2 * HC_COLS:(gi + 1) * 2 * HC_COLS]
        p_c.append(blk.reshape(1, b, w, 2, H_C, HEAD_DIM))
    s_c = [jnp.transpose(ns, (0, 4, 1, 2, 3))[None] for ns in new_states]
    return (hp.reshape(b, s, d), hs.reshape(bd, 1, d),
            p_ckv.reshape(1, b, s, KV_LORA), p_kr.reshape(1, b, s, ROPE_D),
            p_fkv.reshape(1, b, s, 2, KV_B, HEAD_DIM), p_lf.reshape(1, b, s, H_B),
            p_c[0], p_c[1], p_c[2],
            s_ckv.reshape(1, bd, 1, KV_LORA), s_kr[:, :ROPE_D].reshape(1, bd, 1, ROPE_D),
            s_fkv.reshape(1, bd, 1, 2, KV_B, HEAD_DIM), s_lf[:, :H_B].reshape(1, bd, 1, H_B),
            s_c[0], s_c[1], s_c[2])
```

```python
import functools

import numpy as np
import jax
import jax.numpy as jnp
from jax import lax
from jax.experimental import pallas as pl
from jax.experimental.pallas import tpu as pltpu

F32 = jnp.float32
BF16 = jnp.bfloat16
I32 = jnp.int32

D_MODEL = 1024
PAGE = 128
HEAD_DIM = 64
H_A = 8
Q_LORA = 384
KV_LORA = 256
NOPE = 64
ROPE_D = 32
V_DIM = 64
H_B = 8
KV_B = 4
H_C = 8
C_WINDOWS = (128, 512, 2048)
C_DILATIONS = (1, 4, 16)
N_EXPERTS = 32
TOP_K = 4
D_FF = 1024
SWIGLU_ALPHA = 1.702
SWIGLU_LIMIT = 7.0
ROPE_THETA = 10000.0
NORM_EPS = 1e-6
NEG = -1e30
MLA_SCALE = (NOPE + ROPE_D) ** -0.5
HD_SCALE = HEAD_DIM ** -0.5

LANES = 128
SUBLANES = 8
VMEM_LIMIT = 56 * 1024 * 1024
ROW_TILE_CHUNKS = D_MODEL // LANES
FLASH_TILE = 1024
MOE_ROWS = 512
DMA_UNROLL = 8


def _cparams(sem):
    return pltpu.CompilerParams(dimension_semantics=sem, vmem_limit_bytes=VMEM_LIMIT)


def _dot(a, b):
    return jnp.dot(a.astype(BF16), b.astype(BF16), preferred_element_type=F32)


def _dot_nt(a, b):
    return lax.dot_general(a.astype(BF16), b.astype(BF16), (((1,), (1,)), ((), ())),
                           preferred_element_type=F32)


def _split3(x):
    hi = x.astype(BF16)
    r = x - hi.astype(F32)
    mid = r.astype(BF16)
    lo = (r - mid.astype(F32)).astype(BF16)
    return hi, mid, lo


def _dot3(a, b):
    hi, mid, lo = _split3(a)
    return (jnp.dot(hi, b, preferred_element_type=F32) + jnp.dot(mid, b, preferred_element_type=F32)
            + jnp.dot(lo, b, preferred_element_type=F32))


def _modulate(x, g, scale, shift):
    ms = jnp.mean(x * x, axis=-1, keepdims=True)
    return (x * lax.rsqrt(ms + NORM_EPS) * g) * (1.0 + scale) + shift


def _group_rms(x, s_blk, inv_cnt):
    sq = (x * x).astype(BF16)
    parts = [jnp.dot(sq[:, c * LANES:(c + 1) * LANES], s_blk, preferred_element_type=F32)
             for c in range(x.shape[1] // LANES)]
    ssq = parts[0] if len(parts) == 1 else jnp.concatenate(parts, axis=1)
    return lax.rsqrt(ssq * inv_cnt + NORM_EPS)


def _tile_lanes(x, n):
    return jnp.concatenate([x] * n, axis=1)


def _lane_iota(shape):
    return lax.broadcasted_iota(I32, shape, len(shape) - 1)


def _adaln_kernel(c_ref, w_ref, b_ref, o_ref):
    c = c_ref[...]
    s = c * jax.nn.sigmoid(c)
    o_ref[0] = _dot(s, w_ref[0]) + b_ref[0]


def _adaln(c_all, ada_w, ada_b):
    depth, d, n6 = ada_w.shape
    r = c_all.shape[0]
    tn = 768
    return pl.pallas_call(
        _adaln_kernel,
        grid=(depth, n6 // tn),
        in_specs=[pl.BlockSpec((r, d), lambda l, j: (0, 0)),
                  pl.BlockSpec((1, d, tn), lambda l, j: (l, 0, j)),
                  pl.BlockSpec((1, 1, tn), lambda l, j: (l, 0, j))],
        out_specs=pl.BlockSpec((1, r, tn), lambda l, j: (l, 0, j)),
        out_shape=jax.ShapeDtypeStruct((depth, r, n6), F32),
        compiler_params=_cparams(("arbitrary", "arbitrary")),
        name="adaln",
    )(c_all, ada_w, ada_b.reshape(depth, 1, n6))


def _mod_spec(arr, tiles_per_group):
    g, r, d = arr.shape
    if g == 1:
        return pl.BlockSpec((1, r, d), lambda i: (0, 0, 0))
    return pl.BlockSpec((1, r, d), lambda i: (i // tiles_per_group, 0, 0))


def _const_spec(arr):
    nd = arr.ndim
    return pl.BlockSpec(arr.shape, lambda *_: (0,) * nd)


def _cos_sin(pos, half):
    inv_freq = (np.float32(ROPE_THETA) ** (-np.arange(half, dtype=np.float32) / np.float32(half))).astype(np.float32)
    ang = np.asarray(pos, np.float32)[:, None] * inv_freq[None, :]
    return np.cos(ang).astype(np.float32), np.sin(ang).astype(np.float32)


def _l0_consts(w_in, qa_norm, kv_norm, w_qb, w_kvb, q_norm, k_norm, fq_norm, fk_norm, f_bias):
    d = w_in.shape[0]
    z = lambda n: jnp.zeros((d, n), F32)
    cq, ckv = w_in[:, :Q_LORA], w_in[:, Q_LORA:Q_LORA + KV_LORA]
    o = Q_LORA + KV_LORA
    kr = w_in[:, o:o + ROPE_D]
    o += ROPE_D
    fq = w_in[:, o:o + H_B * HEAD_DIM]
    o += H_B * HEAD_DIM
    fk = w_in[:, o:o + KV_B * HEAD_DIM]
    o += KV_B * HEAD_DIM
    fv = w_in[:, o:o + KV_B * HEAD_DIM]
    o += KV_B * HEAD_DIM
    fl = w_in[:, o:o + H_B]
    hr = ROPE_D // 2
    krr = jnp.concatenate([-kr[:, hr:], kr[:, :hr]], axis=1)
    g1 = jnp.concatenate([kr, z(32), kr, z(32)], axis=1)
    g2 = jnp.concatenate([krr, z(32), krr, z(32)], axis=1)
    w_in2 = jnp.concatenate([cq, ckv, g1, g2, fq, fk, fv, fl, z(LANES - H_B)], axis=1).astype(BF16)

    wq = w_qb.reshape(Q_LORA, H_A, NOPE + ROPE_D)
    zq = lambda n: jnp.zeros((Q_LORA, H_A, n), F32)
    rope_c = wq[:, :, NOPE:]
    rope_r = jnp.concatenate([-rope_c[:, :, hr:], rope_c[:, :, :hr]], axis=2)
    wqa = jnp.concatenate([wq[:, :, :NOPE], rope_c, zq(32)], axis=2).reshape(Q_LORA, H_A * LANES)
    wqb = jnp.concatenate([zq(NOPE), rope_r, zq(32)], axis=2).reshape(Q_LORA, H_A * LANES)
    w_q2 = jnp.concatenate([wqa, wqb], axis=1).astype(BF16)

    wkv = w_kvb.reshape(KV_LORA, H_A, NOPE + V_DIM)
    wk = jnp.concatenate([wkv[:, :, :NOPE], jnp.zeros((KV_LORA, H_A, LANES - NOPE), F32)], axis=2)
    w_kv2 = jnp.concatenate([wk.reshape(KV_LORA, H_A * LANES),
                             wkv[:, :, NOPE:].reshape(KV_LORA, H_A * V_DIM)], axis=1).astype(BF16)

    z32 = jnp.zeros((32,), F32)
    qn_r = q_norm[NOPE:]
    ga = jnp.tile(jnp.concatenate([q_norm[:NOPE], qn_r, z32]), H_A) * MLA_SCALE
    gb = jnp.tile(jnp.concatenate([jnp.zeros((NOPE,), F32), qn_r[hr:], qn_r[:hr], z32]), H_A) * MLA_SCALE
    gk = jnp.tile(jnp.concatenate([k_norm[:NOPE], jnp.zeros((LANES - NOPE,), F32)]), H_A)
    kn_r = k_norm[NOPE:]
    kn_rr = jnp.concatenate([kn_r[hr:], kn_r[:hr]])
    gk1 = jnp.concatenate([kn_r, z32, kn_r, z32])
    gk2 = jnp.concatenate([kn_rr, z32, kn_rr, z32])
    vecs = dict(
        qa_norm=qa_norm[None], kv_norm=kv_norm[None], ga=ga[None], gb=gb[None], gk=gk[None],
        gk1=gk1[None], gk2=gk2[None],
        gfq=(jnp.tile(fq_norm, H_B) * HD_SCALE)[None], gfk=jnp.tile(fk_norm, KV_B)[None],
        fbias=jnp.concatenate([f_bias, jnp.zeros((LANES - H_B,), F32)])[None],
    )

    li = np.arange(LANES)
    sq = ((li[:, None] < 64) & (li[None, :] < 64)) | ((li[:, None] >= 64) & (li[:, None] < 96)
                                                       & (li[None, :] >= 64) & (li[None, :] < 96))
    sk = (li[:, None] < 64) & (li[None, :] < 64)
    sf = (li[:, None] // 64) == (li[None, :] // 64)
    cnt_q = np.where(li < 64, 1.0 / 64, np.where(li < 96, 1.0 / 32, 1.0)).astype(np.float32)
    mats = dict(
        s_q=jnp.asarray(sq, BF16), s_k=jnp.asarray(sk, BF16), s_f=jnp.asarray(sf, BF16),
        cnt_q=jnp.asarray(np.tile(cnt_q, H_A))[None],
    )
    return w_in2, w_q2, w_kv2, vecs, mats


def _l0_common(u, wq2_ref, wkv2_ref, p, cq_t, sq_t, ck_t, sk_t):
    cq = u[:, :Q_LORA]
    ckv = u[:, Q_LORA:640]
    g1 = u[:, 640:768]
    g2 = u[:, 768:896]
    fq = u[:, 896:1408]
    fk = u[:, 1408:1664]
    fv = u[:, 1664:1920]
    fl = u[:, 1920:2048]

    cq_n = cq * lax.rsqrt(jnp.mean(cq * cq, axis=-1, keepdims=True) + NORM_EPS) * p["qa_norm"][...]
    q2 = _dot(cq_n, wq2_ref[...])
    qa, qb = q2[:, :H_A * LANES], q2[:, H_A * LANES:]
    rq = _group_rms(qa, p["s_q"][...], p["cnt_q"][...])
    q_mla = rq * (qa * p["ga"][...] * _tile_lanes(cq_t, H_A) + qb * p["gb"][...] * _tile_lanes(sq_t, H_A))

    ckv_n = ckv * lax.rsqrt(jnp.mean(ckv * ckv, axis=-1, keepdims=True) + NORM_EPS) * p["kv_norm"][...]
    kv2 = _dot(ckv_n, wkv2_ref[...])
    kk, v_mla = kv2[:, :H_A * LANES], kv2[:, H_A * LANES:]
    rk = _group_rms(kk, p["s_k"][...], 1.0 / NOPE)
    k_nope = kk * rk * p["gk"][...]

    lane = _lane_iota(g1.shape)
    ss = jnp.sum(jnp.where(lane < ROPE_D, g1 * g1, 0.0), axis=-1, keepdims=True)
    r_kr = lax.rsqrt(ss * (1.0 / ROPE_D) + NORM_EPS)
    kr128 = r_kr * (g1 * p["gk1"][...] * ck_t + g2 * p["gk2"][...] * sk_t)

    rfq = _group_rms(fq, p["s_f"][...], 1.0 / HEAD_DIM)
    fq_n = fq * rfq * p["gfq"][...]
    rfk = _group_rms(fk, p["s_f"][...], 1.0 / HEAD_DIM)
    fk_n = fk * rfk * p["gfk"][...]
    xl = fl + p["fbias"][...]
    logf = jnp.minimum(xl, 0.0) - jnp.log(1.0 + jnp.exp(-jnp.abs(xl)))
    return q_mla, ckv_n, k_nope, v_mla, kr128, fq_n, fk_n, fv, logf


_L0_VEC_NAMES = ("qa_norm", "kv_norm", "ga", "gb", "gk", "gk1", "gk2", "gfq", "gfk", "fbias")
_L0_MAT_NAMES = ("s_q", "s_k", "s_f", "cnt_q")


def _l0_prompt_kernel(tiles_per_seq, x_ref, shift_ref, scale_ref, g_ref, win_ref, wq2_ref, wkv2_ref,
                      cq_ref, sq_ref, ck_ref, sk_ref, eq_ref, ek_ref, ev_ref, pcf_ref, ltri_ref, ones_ref,
                      *rest):
    nv, nm = len(_L0_VEC_NAMES), len(_L0_MAT_NAMES)
    p = dict(zip(_L0_VEC_NAMES + _L0_MAT_NAMES, rest[:nv + nm]))
    q_out, k_out, v_out, ckv_out, kr_out, fkv_out, lf_out, carry = rest[nv + nm:]

    @pl.when(pl.program_id(0) % tiles_per_seq == 0)
    def _():
        carry[...] = jnp.zeros_like(carry)

    xm = _modulate(x_ref[...], g_ref[...], scale_ref[0], shift_ref[0])
    u = _dot(xm, win_ref[...])
    q_mla, ckv_n, k_nope, v_mla, kr128, fq_n, fk_n, fv, logf = _l0_common(
        u, wq2_ref, wkv2_ref, p, cq_ref[...], sq_ref[...], ck_ref[...], sk_ref[...])

    lane = _lane_iota(kr128.shape)
    k_mla = k_nope + _tile_lanes(jnp.where(lane >= NOPE, kr128, 0.0), H_A)

    ltri = ltri_ref[...]
    cf = _dot3_left(ltri, logf) + carry[...]
    carry[...] = cf[cf.shape[0] - 1:, :]
    nh, nm_, nl = _split3(-cf)
    bias = jnp.dot(jnp.concatenate([nh, nm_, nl], axis=1), pcf_ref[...], preferred_element_type=F32)

    q_fox = _dot(fq_n, eq_ref[...]) + ones_ref[...]
    k_fox = _dot(fk_n, ek_ref[...]) + bias
    v_fox = _dot(fv, ev_ref[...])

    q_out[...] = jnp.concatenate([q_mla, q_fox], axis=1).astype(BF16)
    k_out[...] = jnp.concatenate([k_mla, k_fox], axis=1).astype(BF16)
    v_out[...] = jnp.concatenate([v_mla, v_fox], axis=1).astype(BF16)
    ckv_out[...] = ckv_n
    kr_out[...] = kr128[:, :ROPE_D]
    fkv_out[...] = jnp.concatenate([fk_n, fv], axis=1)
    lf_out[...] = logf[:, :H_B]


def _dot3_left(m01, x):
    hi, mid, lo = _split3(x)
    return (jnp.dot(m01, hi, preferred_element_type=F32) + jnp.dot(m01, mid, preferred_element_type=F32)
            + jnp.dot(m01, lo, preferred_element_type=F32))


def _fox_place_mats():
    eq = np.zeros((H_B * HEAD_DIM, H_B * LANES), np.float32)
    ek = np.zeros((KV_B * HEAD_DIM, H_B * LANES), np.float32)
    ev = np.zeros((KV_B * HEAD_DIM, KV_B * LANES), np.float32)
    pcf = np.zeros((3 * LANES, H_B * LANES), np.float32)
    ones = np.zeros((1, H_B * LANES), np.float32)
    dd = np.arange(HEAD_DIM)
    for h in range(H_B):
        eq[h * HEAD_DIM + dd, h * LANES + dd] = 1.0
        ek[(h // 2) * HEAD_DIM + dd, h * LANES + dd] = 1.0
        for s in range(3):
            pcf[s * LANES + h, h * LANES + HEAD_DIM + s] = 1.0
            ones[0, h * LANES + HEAD_DIM + s] = 1.0
    for kvh in range(KV_B):
        for g in range(2):
            ev[kvh * HEAD_DIM + dd, kvh * LANES + g * HEAD_DIM + dd] = 1.0
    return (jnp.asarray(eq, BF16), jnp.asarray(ek, BF16), jnp.asarray(ev, BF16), jnp.asarray(pcf, BF16),
            jnp.asarray(ones, F32))


def _l0_tables(pos):
    c0, s0 = _cos_sin(pos, ROPE_D // 2)
    c = np.concatenate([c0] * 2, axis=1)
    s = np.concatenate([s0] * 2, axis=1)
    n = c.shape[0]
    one = np.ones((n, NOPE), np.float32)
    z32 = np.zeros((n, 32), np.float32)
    z64 = np.zeros((n, NOPE), np.float32)
    cq = np.concatenate([one, c, z32], axis=1)
    sq = np.concatenate([z64, s, z32], axis=1)
    ck = np.concatenate([c, z32, c, z32], axis=1)
    sk = np.concatenate([s, z32, s, z32], axis=1)
    return tuple(jnp.asarray(t) for t in (cq, sq, ck, sk))


def _l0_prompt_proj(x, shift, scale, g, consts, seq):
    n, d = x.shape
    tm = 256
    tps = seq // tm
    w_in2, w_q2, w_kv2, vecs, mats = consts
    cq, sq, ck, sk = _l0_tables(np.arange(seq))
    eq, ek, ev, pcf, ones = _fox_place_mats()
    ltri = jnp.asarray(np.tril(np.ones((tm, tm), np.float32)), BF16)
    tab_spec = pl.BlockSpec((tm, LANES), lambda i: (i % tps, 0))
    row = lambda w: pl.BlockSpec((tm, w), lambda i: (i, 0))
    small = [vecs[k] for k in _L0_VEC_NAMES] + [mats[k] for k in _L0_MAT_NAMES]
    ins = [x, shift, scale, g, w_in2, w_q2, w_kv2, cq, sq, ck, sk, eq, ek, ev, pcf, ltri, ones] + small
    in_specs = ([row(d), _mod_spec(shift, tps), _mod_spec(scale, tps), _const_spec(g), _const_spec(w_in2),
                 _const_spec(w_q2), _const_spec(w_kv2), tab_spec, tab_spec, tab_spec, tab_spec,
                 _const_spec(eq), _const_spec(ek), _const_spec(ev), _const_spec(pcf), _const_spec(ltri),
                 _const_spec(ones)] + [_const_spec(a) for a in small])
    widths = (2 * H_A * LANES, 2 * H_A * LANES, H_A * V_DIM + KV_B * LANES, KV_LORA, ROPE_D,
              2 * KV_B * HEAD_DIM, H_B)
    dtypes = (BF16, BF16, BF16, F32, F32, F32, F32)
    return pl.pallas_call(
        functools.partial(_l0_prompt_kernel, tps),
        grid=(n // tm,),
        in_specs=in_specs,
        out_specs=[row(w) for w in widths],
        out_shape=[jax.ShapeDtypeStruct((n, w), dt) for w, dt in zip(widths, dtypes)],
        scratch_shapes=[pltpu.VMEM((1, LANES), F32)],
        compiler_params=_cparams(("arbitrary",)),
        name="l0_prompt_proj",
    )(*ins)


def _l0_sample_kernel(x_ref, shift_ref, scale_ref, g_ref, win_ref, wq2_ref, wkv2_ref,
                      cq_ref, sq_ref, ck_ref, sk_ref, *rest):
    nv, nm = len(_L0_VEC_NAMES), len(_L0_MAT_NAMES)
    p = dict(zip(_L0_VEC_NAMES + _L0_MAT_NAMES, rest[:nv + nm]))
    q_out, ckv_out, kr_out, fq_out, fkv_out, lf_out = rest[nv + nm:]
    xm = _modulate(x_ref[...], g_ref[...], scale_ref[0], shift_ref[0])
    u = _dot(xm, win_ref[...])
    q_mla, ckv_n, _, _, kr128, fq_n, fk_n, fv, logf = _l0_common(
        u, wq2_ref, wkv2_ref, p, cq_ref[...], sq_ref[...], ck_ref[...], sk_ref[...])
    q_out[...] = q_mla
    ckv_out[...] = ckv_n
    kr_out[...] = kr128
    fq_out[...] = fq_n
    fkv_out[...] = jnp.concatenate([fk_n, fv], axis=1)
    lf_out[...] = logf


def _l0_sample_proj(x, shift, scale, g, consts, past):
    n, d = x.shape
    w_in2, w_q2, w_kv2, vecs, mats = consts
    tabs = _l0_tables(np.full((1,), past))
    small = [vecs[k] for k in _L0_VEC_NAMES] + [mats[k] for k in _L0_MAT_NAMES]
    ins = [x, shift, scale, g, w_in2, w_q2, w_kv2, *tabs] + small
    widths = (H_A * LANES, KV_LORA, LANES, H_B * HEAD_DIM, 2 * KV_B * HEAD_DIM, LANES)
    return pl.pallas_call(
        _l0_sample_kernel,
        grid=(1,),
        in_specs=[_const_spec(a) for a in ins],
        out_specs=[pl.BlockSpec((n, w), lambda i: (0, 0)) for w in widths],
        out_shape=[jax.ShapeDtypeStruct((n, w), F32) for w in widths],
        compiler_params=_cparams(("arbitrary",)),
        name="l0_sample_proj",
    )(*ins)


def _flash_kernel(tq, qi_ref, kj_ref, q_ref, k_ref, v_ref, o_ref, m_scr, l_scr, acc_scr):
    t = pl.program_id(2)
    i, j = qi_ref[t], kj_ref[t]
    tk = tq

    @pl.when(j == 0)
    def _():
        m_scr[...] = jnp.full(m_scr.shape, NEG, F32)
        l_scr[...] = jnp.zeros_like(l_scr)
        acc_scr[...] = jnp.zeros_like(acc_scr)

    def step(masked):
        q = q_ref[0]
        k = k_ref[0]
        v = v_ref[0]
        lane = _lane_iota((tq, LANES))
        new_acc = []
        for h in range(2):
            s = _dot_nt(q[:, h * LANES:(h + 1) * LANES], k[:, h * LANES:(h + 1) * LANES])
            if masked:
                row = lax.broadcasted_iota(I32, (tq, tk), 0)
                col = lax.broadcasted_iota(I32, (tq, tk), 1)
                s = jnp.where(col <= row, s, NEG)
            m_prev = m_scr[h]
            m_new = jnp.maximum(m_prev, jnp.max(s, axis=-1, keepdims=True))
            alpha = jnp.exp(m_prev - m_new)
            pr = jnp.exp(s - m_new[:, :1])
            l_scr[h] = alpha * l_scr[h] + jnp.sum(pr, axis=-1, keepdims=True)
            m_scr[h] = m_new
            new_acc.append(alpha * acc_scr[...] + _dot(pr, v))
        acc_scr[...] = jnp.where(lane < V_DIM, new_acc[0], new_acc[1])

    @pl.when(j < i)
    def _():
        step(False)

    @pl.when(j == i)
    def _():
        step(True)
        lane = _lane_iota((tq, LANES))
        l = jnp.where(lane < V_DIM, l_scr[0], l_scr[1])
        o_ref[0] = (acc_scr[...] / l).astype(o_ref.dtype)


def _flash_attention(q, k, v):
    b, s, _ = q.shape
    tq = FLASH_TILE if s % FLASH_TILE == 0 else 512
    nq = s // tq
    npair = H_A // 2 + H_B // 2
    qi = np.concatenate([np.full(i + 1, i) for i in range(nq)]).astype(np.int32)
    kj = np.concatenate([np.arange(i + 1) for i in range(nq)]).astype(np.int32)
    gs = pltpu.PrefetchScalarGridSpec(
        num_scalar_prefetch=2,
        grid=(b, npair, qi.shape[0]),
        in_specs=[pl.BlockSpec((1, tq, 2 * LANES), lambda bb, p, t, qi_, kj_: (bb, qi_[t], p)),
                  pl.BlockSpec((1, tq, 2 * LANES), lambda bb, p, t, qi_, kj_: (bb, kj_[t], p)),
                  pl.BlockSpec((1, tq, LANES), lambda bb, p, t, qi_, kj_: (bb, kj_[t], p))],
        out_specs=pl.BlockSpec((1, tq, LANES), lambda bb, p, t, qi_, kj_: (bb, qi_[t], p)),
        scratch_shapes=[pltpu.VMEM((2, tq, LANES), F32), pltpu.VMEM((2, tq, LANES), F32),
                        pltpu.VMEM((tq, LANES), F32)],
    )
    return pl.pallas_call(
        functools.partial(_flash_kernel, tq),
        grid_spec=gs,
        out_shape=jax.ShapeDtypeStruct((b, s, npair * LANES), BF16),
        compiler_params=_cparams(("arbitrary",) * 3),
        name="l0_flash",
    )(jnp.asarray(qi), jnp.asarray(kj), q, k, v)


def _router_epilogue(h1, g2, shift2, scale2, wr_ref, br_ref, xm_ref, gate_ref, eidx_ref):
    tm = h1.shape[0]
    xm = _modulate(h1, g2, scale2, shift2)
    for c in range(ROW_TILE_CHUNKS):
        xm_ref[pl.ds(c, tm, stride=SUBLANES), :] = xm[:, c * LANES:(c + 1) * LANES]
    logits = jnp.dot(xm, wr_ref[...], preferred_element_type=F32, precision=lax.Precision.HIGHEST) + br_ref[...]
    lane = _lane_iota(logits.shape)
    x = logits
    vals, ev = [], jnp.zeros(logits.shape, I32)
    for kk in range(TOP_K):
        m = jnp.max(x, axis=-1, keepdims=True)
        idx = jnp.min(jnp.where(x == m, lane, LANES), axis=-1, keepdims=True)
        vals.append(m)
        ev = jnp.where(lane == kk, idx, ev)
        x = jnp.where(lane == idx, -3e38, x)
    es = [jnp.exp(vv - vals[0]) for vv in vals]
    tot = es[0] + es[1] + es[2] + es[3]
    gv = jnp.zeros(logits.shape, F32)
    for kk in range(TOP_K):
        gv = jnp.where(lane == kk, es[kk] / tot, gv)
    gate_ref[...] = gv
    eidx_ref[...] = ev


def _out_router_kernel(a_ref, wo_ref, res_ref, gate1_ref, g2_ref, shift2_ref, scale2_ref, wr_ref, br_ref,
                       h1_ref, xm_ref, gate_ref, eidx_ref):
    h1 = res_ref[...] + gate1_ref[0] * _dot(a_ref[...], wo_ref[...])
    h1_ref[...] = h1
    _router_epilogue(h1, g2_ref[...], shift2_ref[0], scale2_ref[0], wr_ref, br_ref, xm_ref, gate_ref, eidx_ref)


def _router_consts(router_w, router_b):
    d, e = router_w.shape
    wr = jnp.concatenate([router_w, jnp.zeros((d, LANES - e), F32)], axis=1)
    br = jnp.concatenate([router_b, jnp.full((LANES - e,), NEG, F32)])[None]
    return wr, br


def _router_out_specs(n, tm):
    specs = [pl.BlockSpec((tm, D_MODEL), lambda i: (i, 0)),
             pl.BlockSpec((tm * SUBLANES, LANES), lambda i: (i, 0)),
             pl.BlockSpec((tm, LANES), lambda i: (i, 0)),
             pl.BlockSpec((tm, LANES), lambda i: (i, 0))]
    shapes = [jax.ShapeDtypeStruct((n, D_MODEL), F32), jax.ShapeDtypeStruct((n * SUBLANES, LANES), F32),
              jax.ShapeDtypeStruct((n, LANES), F32), jax.ShapeDtypeStruct((n, LANES), I32)]
    return specs, shapes


def _out_router(a, w_out, res, gate1, g2, shift2, scale2, wr, br, rows_per_group):
    n, ka = a.shape
    tm = min(256, n)
    tpg = max(rows_per_group // tm, 1)
    wo = w_out.astype(BF16)
    out_specs, out_shapes = _router_out_specs(n, tm)
    return pl.pallas_call(
        _out_router_kernel,
        grid=(n // tm,),
        in_specs=[pl.BlockSpec((tm, ka), lambda i: (i, 0)), _const_spec(wo),
                  pl.BlockSpec((tm, D_MODEL), lambda i: (i, 0)), _mod_spec(gate1, tpg), _const_spec(g2),
                  _mod_spec(shift2, tpg), _mod_spec(scale2, tpg), _const_spec(wr), _const_spec(br)],
        out_specs=out_specs,
        out_shape=out_shapes,
        compiler_params=_cparams(("arbitrary",)),
        name="out_router",
    )(a, wo, res, gate1, g2, shift2, scale2, wr, br)


def _moe_plan(eidx, tb):
    n = eidx.shape[0]
    m = n * TOP_K
    flat_e = eidx[:, :TOP_K].reshape(m)
    onehot = (flat_e[:, None] == jnp.arange(N_EXPERTS, dtype=I32)[None, :]).astype(I32)
    csum = jnp.cumsum(onehot, axis=0)
    counts = csum[-1]
    padded = (counts + tb - 1) // tb * tb
    pend = jnp.cumsum(padded)
    pstart = pend - padded
    pos = jnp.sum(onehot * (csum - 1 + pstart[None, :]), axis=1)
    nblk = -(-m // tb) + N_EXPERTS
    first_row = jnp.arange(nblk, dtype=I32) * tb
    blk_e = jnp.minimum(jnp.sum((pend[None, :] <= first_row[:, None]).astype(I32), axis=1), N_EXPERTS - 1)
    row_tok = jnp.zeros((nblk * tb,), I32).at[pos].set(jnp.arange(m, dtype=I32) // TOP_K)
    return pos.astype(I32), blk_e, row_tok, nblk


def _gather_rows(idx_ref, base, src_hbm, dst, sem, nrow):
    def body(g, c):
        for u in range(DMA_UNROLL):
            r = g * DMA_UNROLL + u
            t = idx_ref[base + r]
            pltpu.make_async_copy(src_hbm.at[pl.ds(t * SUBLANES, SUBLANES)],
                                  dst.at[pl.ds(r * SUBLANES, SUBLANES)], sem).start()
        return c
    lax.fori_loop(0, nrow // DMA_UNROLL, body, 0)


def _wait_rows(src_hbm, dst, sem):
    pltpu.make_async_copy(src_hbm.at[pl.ds(0, dst.shape[0])], dst, sem).wait()


def _expert_kernel(tb, blk_e_ref, tok_ref, x_hbm, w1_ref, b1_ref, w2_ref, b2_ref, y_ref, xg0, xg1,
                   w1b, w2b, sem):
    i, nb = pl.program_id(0), pl.num_programs(0)
    bufs = (xg0, xg1)

    @pl.when((i == 0) | (blk_e_ref[i] != blk_e_ref[jnp.maximum(i - 1, 0)]))
    def _():
        w1b[...] = w1_ref[0, 0].astype(BF16)
        w2b[...] = w2_ref[0, 0].astype(BF16)

    @pl.when(i == 0)
    def _():
        _gather_rows(tok_ref, 0, x_hbm, xg0, sem.at[0], tb)

    def run(slot):
        @pl.when(i + 1 < nb)
        def _():
            _gather_rows(tok_ref, (i + 1) * tb, x_hbm, bufs[1 - slot], sem.at[1 - slot], tb)

        xg = bufs[slot]
        _wait_rows(x_hbm, xg, sem.at[slot])
        x = jnp.concatenate([xg[pl.ds(c, tb, stride=SUBLANES), :] for c in range(ROW_TILE_CHUNKS)], axis=1)
        hcat = _dot(x, w1b[...]) + b1_ref[0, 0]
        glu = jnp.minimum(hcat[:, :D_FF], SWIGLU_LIMIT)
        lin = jnp.clip(hcat[:, D_FF:], -SWIGLU_LIMIT, SWIGLU_LIMIT)
        act = glu * jax.nn.sigmoid(SWIGLU_ALPHA * glu) * (lin + 1.0)
        y = _dot(act, w2b[...]) + b2_ref[0, 0]
        for c in range(ROW_TILE_CHUNKS):
            y_ref[pl.ds(c, tb, stride=SUBLANES), :] = y[:, c * LANES:(c + 1) * LANES]

    @pl.when(i % 2 == 0)
    def _():
        run(0)

    @pl.when(i % 2 == 1)
    def _():
        run(1)


def _experts(xm_tiles, blk_e, row_tok, nblk, tb, layer, w1, b1, w2, b2):
    gs = pltpu.PrefetchScalarGridSpec(
        num_scalar_prefetch=2,
        grid=(nblk,),
        in_specs=[pl.BlockSpec(memory_space=pl.ANY),
                  pl.BlockSpec((1, 1, D_MODEL, 2 * D_FF), lambda i, be, tk: (layer, be[i], 0, 0)),
                  pl.BlockSpec((1, 1, 1, 2 * D_FF), lambda i, be, tk: (layer, be[i], 0, 0)),
                  pl.BlockSpec((1, 1, D_FF, D_MODEL), lambda i, be, tk: (layer, be[i], 0, 0)),
                  pl.BlockSpec((1, 1, 1, D_MODEL), lambda i, be, tk: (layer, be[i], 0, 0))],
        out_specs=pl.BlockSpec((tb * SUBLANES, LANES), lambda i, be, tk: (i, 0)),
        scratch_shapes=[pltpu.VMEM((tb * SUBLANES, LANES), F32), pltpu.VMEM((tb * SUBLANES, LANES), F32),
                        pltpu.VMEM((D_MODEL, 2 * D_FF), BF16), pltpu.VMEM((D_FF, D_MODEL), BF16),
                        pltpu.SemaphoreType.DMA((2,))],
    )
    depth, ne = b1.shape[:2]
    return pl.pallas_call(
        functools.partial(_expert_kernel, tb),
        grid_spec=gs,
        out_shape=jax.ShapeDtypeStruct((nblk * tb * SUBLANES, LANES), F32),
        compiler_params=_cparams(("arbitrary",)),
        name="moe_experts",
    )(blk_e, row_tok, xm_tiles, w1, b1.reshape(depth, ne, 1, 2 * D_FF), w2,
      b2.reshape(depth, ne, 1, D_MODEL))


def _combine_kernel(tc, pos_ref, y_hbm, gates_ref, h1_ref, gate2_ref, o_ref, buf0, buf1, sem):
    nrow = TOP_K * tc
    i, nb = pl.program_id(0), pl.num_programs(0)
    bufs = (buf0, buf1)

    @pl.when(i == 0)
    def _():
        _gather_rows(pos_ref, 0, y_hbm, buf0, sem.at[0], nrow)

    def run(slot):
        @pl.when(i + 1 < nb)
        def _():
            _gather_rows(pos_ref, (i + 1) * nrow, y_hbm, bufs[1 - slot], sem.at[1 - slot], nrow)

        buf = bufs[slot]
        _wait_rows(y_hbm, buf, sem.at[slot])
        gates = gates_ref[...]
        cols = []
        for c in range(ROW_TILE_CHUNKS):
            acc = jnp.zeros((tc, LANES), F32)
            for kk in range(TOP_K):
                yk = buf[pl.ds(kk * tc * SUBLANES + c, tc, stride=SUBLANES), :]
                acc = acc + yk * gates[:, kk:kk + 1]
            cols.append(acc)
        moe = jnp.concatenate(cols, axis=1)
        o_ref[...] = h1_ref[...] + gate2_ref[0] * moe

    @pl.when(i % 2 == 0)
    def _():
        run(0)

    @pl.when(i % 2 == 1)
    def _():
        run(1)


def _combine(pos, ys, gates, h1, gate2, rows_per_group):
    n = h1.shape[0]
    tc = min(256, n)
    nt = n // tc
    tpg = max(rows_per_group // tc, 1)
    pos_flat = pos.reshape(nt, tc, TOP_K).transpose(0, 2, 1).reshape(nt * TOP_K * tc)
    nbuf = TOP_K * tc * SUBLANES
    g, r, d = gate2.shape
    gate_spec = (pl.BlockSpec((1, r, d), lambda i, ps: (0, 0, 0)) if g == 1
                 else pl.BlockSpec((1, r, d), lambda i, ps: (i // tpg, 0, 0)))
    gs = pltpu.PrefetchScalarGridSpec(
        num_scalar_prefetch=1,
        grid=(nt,),
        in_specs=[pl.BlockSpec(memory_space=pl.ANY),
                  pl.BlockSpec((tc, LANES), lambda i, ps: (i, 0)),
                  pl.BlockSpec((tc, D_MODEL), lambda i, ps: (i, 0)),
                  gate_spec],
        out_specs=pl.BlockSpec((tc, D_MODEL), lambda i, ps: (i, 0)),
        scratch_shapes=[pltpu.VMEM((nbuf, LANES), F32), pltpu.VMEM((nbuf, LANES), F32),
                        pltpu.SemaphoreType.DMA((2,))],
    )
    return pl.pallas_call(
        functools.partial(_combine_kernel, tc),
        grid_spec=gs,
        out_shape=jax.ShapeDtypeStruct((n, D_MODEL), F32),
        compiler_params=_cparams(("arbitrary",)),
        name="moe_combine",
    )(pos_flat, ys, gates, h1, gate2)


def _moe(h1, xm_tiles, gates, eidx, gate2, rows_per_group, layer, w1, b1, w2, b2):
    n = h1.shape[0]
    tb = MOE_ROWS if n * TOP_K >= 16 * MOE_ROWS else PAGE
    pos, blk_e, row_tok, nblk = _moe_plan(eidx, tb)
    ys = _experts(xm_tiles, blk_e, row_tok, nblk, tb, layer, w1, b1, w2, b2)
    return _combine(pos, ys, gates, h1, gate2, rows_per_group)


PAGES_PER_STEP = 32


def _softmax_step(s, m_scr, l_scr):
    m_prev = m_scr[...]
    m_new = jnp.maximum(m_prev, jnp.max(s, axis=-1, keepdims=True))
    alpha = jnp.exp(m_prev - m_new)
    pr = jnp.exp(s - m_new[:, :1])
    l_scr[...] = alpha * l_scr[...] + jnp.sum(pr, axis=-1, keepdims=True)
    m_scr[...] = m_new
    return alpha[:, :1], pr


def _mla_decode_kernel(npg, pt_ref, q_ref, qcol_ref, cnew_ref, krnew_ref, gk_ref, gkcol_ref, t64_ref, mask_ref,
                       wkbt_ref, wvb_ref, *rest):
    ckv_refs, kr_refs = rest[:npg], rest[npg:2 * npg]
    o_ref, m_scr, l_scr, acc_scr = rest[2 * npg:]
    del pt_ref
    c = pl.program_id(1)

    @pl.when(c == 0)
    def _():
        m_scr[...] = jnp.full(m_scr.shape, NEG, F32)
        l_scr[...] = jnp.zeros_like(l_scr)
        acc_scr[...] = jnp.zeros_like(acc_scr)

    q8 = q_ref[0]
    mask = mask_ref[...]
    qr = q8[:, NOPE:NOPE + ROPE_D]
    cc = jnp.concatenate([r[0] for r in ckv_refs], axis=0).astype(BF16)
    kt = _dot_nt(wkbt_ref[...], cc)
    qg = qcol_ref[0] * gkcol_ref[...]
    ntile = kt.shape[1] // LANES
    row8 = lax.broadcasted_iota(I32, (SUBLANES, kt.shape[1]), 0)
    ssq = jnp.zeros((SUBLANES, kt.shape[1]), F32)
    tt = jnp.zeros((SUBLANES, kt.shape[1]), F32)
    for h in range(H_A):
        blk = kt[h * NOPE:(h + 1) * NOPE, :]
        qh = _tile_lanes(qg[h * NOPE:(h + 1) * NOPE, :], ntile)
        ssq = jnp.where(row8 == h, jnp.sum(blk * blk, axis=0, keepdims=True), ssq)
        tt = jnp.where(row8 == h, jnp.sum(blk * qh, axis=0, keepdims=True), tt)
    krt = jnp.concatenate([r[0] for r in kr_refs], axis=1)
    s = tt * lax.rsqrt(ssq * (1.0 / NOPE) + NORM_EPS) + _dot(qr, krt)
    alpha, pr = _softmax_step(s, m_scr, l_scr)
    acc_scr[...] = alpha * acc_scr[...] + _dot(pr, cc)

    @pl.when(c == pl.num_programs(1) - 1)
    def _():
        cnew = cnew_ref[0]
        qmat = _dot(q8[:, :NOPE] * gk_ref[...], t64_ref[...]) * mask
        knew = _dot_nt(jnp.broadcast_to(cnew, (SUBLANES, KV_LORA)), wkbt_ref[...])
        tt_n = jnp.sum(qmat * knew, axis=-1, keepdims=True)
        ssq_n = jnp.sum(mask * knew * knew, axis=-1, keepdims=True)
        s_n = (tt_n * lax.rsqrt(ssq_n * (1.0 / NOPE) + NORM_EPS)
               + jnp.sum(qr * krnew_ref[0][:, :ROPE_D], axis=-1, keepdims=True))
        alpha_n, p_n = _softmax_step(s_n, m_scr, l_scr)
        acc = alpha_n * acc_scr[...] + p_n * cnew
        lat = acc / l_scr[...][:, :1]
        o8 = _dot(lat, wvb_ref[...]) * mask
        o_ref[0] = jnp.sum(o8, axis=0, keepdims=True)


def _mla_decode(page_table, q_s, ckv_s, kr_s, ckv_cache, krt_cache, w_kvb, k_norm):
    bd, n_pages = page_table.shape
    npg = PAGES_PER_STEP
    wkv = w_kvb.reshape(KV_LORA, H_A, NOPE + V_DIM)
    wkbt = wkv[:, :, :NOPE].reshape(KV_LORA, H_A * NOPE).T.astype(BF16)
    wvb = wkv[:, :, NOPE:].reshape(KV_LORA, H_A * V_DIM).astype(BF16)
    hh = np.arange(H_A)[:, None]
    mask = jnp.asarray((np.arange(H_A * NOPE)[None, :] // NOPE) == hh, F32)
    t64 = jnp.asarray(np.tile(np.eye(NOPE, dtype=np.float32), (1, H_A)), BF16)
    gk = k_norm[:NOPE][None]
    gkcol = jnp.broadcast_to(jnp.tile(k_norm[:NOPE], H_A)[:, None], (H_A * NOPE, LANES))
    q8 = q_s.reshape(bd, H_A, LANES)
    qcol = jnp.broadcast_to(q8[:, :, :NOPE].reshape(bd, H_A * NOPE, 1), (bd, H_A * NOPE, LANES))
    consts = [gk, gkcol, t64, mask, wkbt, wvb]

    def page_spec(shape, k):
        return pl.BlockSpec((1,) + shape, lambda b, c, pt: (pt[b, c * npg + k], 0, 0))

    gs = pltpu.PrefetchScalarGridSpec(
        num_scalar_prefetch=1,
        grid=(bd, n_pages // npg),
        in_specs=([pl.BlockSpec((1, H_A, LANES), lambda b, c, pt: (b, 0, 0)),
                   pl.BlockSpec((1, H_A * NOPE, LANES), lambda b, c, pt: (b, 0, 0)),
                   pl.BlockSpec((1, 1, KV_LORA), lambda b, c, pt: (b, 0, 0)),
                   pl.BlockSpec((1, 1, LANES), lambda b, c, pt: (b, 0, 0))]
                  + [pl.BlockSpec(a.shape, lambda b, c, pt: (0, 0)) for a in consts]
                  + [page_spec((PAGE, KV_LORA), k) for k in range(npg)]
                  + [page_spec((ROPE_D, PAGE), k) for k in range(npg)]),
        out_specs=pl.BlockSpec((1, 1, H_A * V_DIM), lambda b, c, pt: (b, 0, 0)),
        scratch_shapes=[pltpu.VMEM((SUBLANES, LANES), F32), pltpu.VMEM((SUBLANES, LANES), F32),
                        pltpu.VMEM((SUBLANES, KV_LORA), F32)],
    )
    out = pl.pallas_call(
        functools.partial(_mla_decode_kernel, npg),
        grid_spec=gs,
        out_shape=jax.ShapeDtypeStruct((bd, 1, H_A * V_DIM), F32),
        compiler_params=_cparams(("arbitrary", "arbitrary")),
        name="l0_mla_decode",
    )(page_table, q8, qcol, ckv_s.reshape(bd, 1, KV_LORA), kr_s.reshape(bd, 1, LANES), *consts,
      *([ckv_cache] * npg), *([krt_cache] * npg))
    return out.reshape(bd, H_A * V_DIM)


def _prefix_lanes(x):
    n = x.shape[1]
    lane = _lane_iota(x.shape)
    sh = 1
    while sh < n:
        x = x + jnp.where(lane >= sh, pltpu.roll(x, sh, axis=1), 0.0)
        sh *= 2
    return x


def _fox_decode_kernel(npg, pt_ref, q_ref, knew_ref, vnew_ref, lfnew_ref, t64_ref, mask_ref,
                       pe_ref, po_ref, *rest):
    kv_refs, lf_refs = rest[:npg], rest[npg:2 * npg]
    o_ref, m_scr, l_scr, acc_scr, carry = rest[2 * npg:]
    del pt_ref
    c = pl.program_id(1)
    nkv = KV_B * HEAD_DIM

    @pl.when(c == 0)
    def _():
        m_scr[...] = jnp.full(m_scr.shape, NEG, F32)
        l_scr[...] = jnp.zeros_like(l_scr)
        acc_scr[...] = jnp.zeros_like(acc_scr)
        carry[...] = jnp.zeros_like(carry)

    mask = mask_ref[...]
    qblk = _dot(q_ref[0], t64_ref[...]) * mask
    kt = jnp.concatenate([r[0, :nkv, :] for r in kv_refs], axis=1)
    vt = jnp.concatenate([r[0, nkv:, :] for r in kv_refs], axis=1)
    lft = jnp.concatenate([r[0] for r in lf_refs], axis=1)
    cf = _prefix_lanes(lft) + carry[...][:, :1]
    carry[...] = jnp.broadcast_to(cf[:, cf.shape[1] - 1:], carry.shape)
    s = _dot(qblk, kt) - cf
    alpha, pr = _softmax_step(s, m_scr, l_scr)
    acc_scr[...] = alpha * acc_scr[...] + _dot_nt(pr, vt)

    @pl.when(c == pl.num_programs(1) - 1)
    def _():
        cf_t = carry[...][:, :1] + lfnew_ref[0][:, :1]
        s_n = jnp.sum(qblk * knew_ref[0], axis=-1, keepdims=True) - cf_t
        alpha_n, p_n = _softmax_step(s_n, m_scr, l_scr)
        acc = (alpha_n * acc_scr[...] + p_n * vnew_ref[0]) / l_scr[...][:, :1] * mask
        row = lax.broadcasted_iota(I32, acc.shape, 0)
        even = jnp.where(row % 2 == 0, acc, 0.0)
        odd = jnp.where(row % 2 == 1, acc, 0.0)
        o8 = _dot(even, pe_ref[...]) + _dot(odd, po_ref[...])
        o_ref[0] = jnp.sum(o8, axis=0, keepdims=True)


def _fox_decode(page_table, fq_s, fkv_s, lf_s, kvt_cache, lft_cache):
    bd, n_pages = page_table.shape
    npg = PAGES_PER_STEP
    nkv = KV_B * HEAD_DIM
    pc = npg * PAGE
    hh = np.arange(H_B)[:, None]
    mask = jnp.asarray((np.arange(nkv)[None, :] // HEAD_DIM) == hh // 2, F32)
    t64 = jnp.asarray(np.tile(np.eye(HEAD_DIM, dtype=np.float32), (1, KV_B)), BF16)
    pe = np.zeros((nkv, H_B * HEAD_DIM), np.float32)
    po = np.zeros((nkv, H_B * HEAD_DIM), np.float32)
    dd = np.arange(HEAD_DIM)
    for j in range(KV_B):
        pe[j * HEAD_DIM + dd, (2 * j) * HEAD_DIM + dd] = 1.0
        po[j * HEAD_DIM + dd, (2 * j + 1) * HEAD_DIM + dd] = 1.0
    consts = [t64, mask, jnp.asarray(pe, BF16), jnp.asarray(po, BF16)]
    lfnew = jnp.broadcast_to(lf_s[:, :H_B, None], (bd, H_B, LANES))

    def page_spec(shape, k):
        return pl.BlockSpec((1,) + shape, lambda b, c, pt: (pt[b, c * npg + k], 0, 0))

    gs = pltpu.PrefetchScalarGridSpec(
        num_scalar_prefetch=1,
        grid=(bd, n_pages // npg),
        in_specs=([pl.BlockSpec((1, H_B, HEAD_DIM), lambda b, c, pt: (b, 0, 0)),
                   pl.BlockSpec((1, 1, nkv), lambda b, c, pt: (b, 0, 0)),
                   pl.BlockSpec((1, 1, nkv), lambda b, c, pt: (b, 0, 0)),
                   pl.BlockSpec((1, H_B, LANES), lambda b, c, pt: (b, 0, 0))]
                  + [pl.BlockSpec(a.shape, lambda b, c, pt: (0, 0)) for a in consts]
                  + [page_spec((2 * nkv, PAGE), k) for k in range(npg)]
                  + [page_spec((H_B, PAGE), k) for k in range(npg)]),
        out_specs=pl.BlockSpec((1, 1, H_B * HEAD_DIM), lambda b, c, pt: (b, 0, 0)),
        scratch_shapes=[pltpu.VMEM((SUBLANES, LANES), F32), pltpu.VMEM((SUBLANES, LANES), F32),
                        pltpu.VMEM((SUBLANES, nkv), F32), pltpu.VMEM((SUBLANES, LANES), F32)],
    )
    out = pl.pallas_call(
        functools.partial(_fox_decode_kernel, npg),
        grid_spec=gs,
        out_shape=jax.ShapeDtypeStruct((bd, 1, H_B * HEAD_DIM), F32),
        compiler_params=_cparams(("arbitrary", "arbitrary")),
        name="l0_fox_decode",
    )(page_table, fq_s.reshape(bd, H_B, HEAD_DIM), fkv_s[:, :nkv].reshape(bd, 1, nkv),
      fkv_s[:, nkv:].reshape(bd, 1, nkv), lfnew, *consts, *([kvt_cache] * npg), *([lft_cache] * npg))
    return out.reshape(bd, H_B * HEAD_DIM)


N_GROUPS = len(C_WINDOWS)
GROUP_COLS = 3 * H_C * HEAD_DIM
HC_COLS = H_C * HEAD_DIM


def _rot_half64(x):
    n = x.shape[1]
    lane = _lane_iota(x.shape)
    fwd = pltpu.roll(x, n - HEAD_DIM // 2, axis=1)
    bwd = pltpu.roll(x, HEAD_DIM // 2, axis=1)
    return jnp.where(lane % HEAD_DIM < HEAD_DIM // 2, -fwd, bwd)


def _l1_qkv(u, g, s_f, gq, gk, cos, sin):
    base = g * GROUP_COLS
    q = u[:, base:base + HC_COLS]
    k = u[:, base + HC_COLS:base + 2 * HC_COLS]
    v = u[:, base + 2 * HC_COLS:base + 3 * HC_COLS]
    qn = q * _group_rms(q, s_f, 1.0 / HEAD_DIM) * gq
    kn = k * _group_rms(k, s_f, 1.0 / HEAD_DIM) * gk
    qn = qn * cos + _rot_half64(qn) * sin
    kn = kn * cos + _rot_half64(kn) * sin
    return qn, kn, v


def _l1_tables(pos):
    c, s = _cos_sin(pos, HEAD_DIM // 2)
    return jnp.asarray(np.concatenate([c] * 4, axis=1)), jnp.asarray(np.concatenate([s] * 4, axis=1))


def _l1_prompt_kernel(tm, x_ref, shift_ref, scale_ref, g_ref, w_ref, sf_ref, gq_ref, gk_ref, cos_ref, sin_ref,
                      *rest):
    outs, kvlast_ref, scr = rest[:3 * N_GROUPS], rest[3 * N_GROUPS], rest[3 * N_GROUPS + 1]
    xm = _modulate(x_ref[...], g_ref[...], scale_ref[0], shift_ref[0])
    u = _dot(xm, w_ref[...])
    cos = _tile_lanes(cos_ref[...], HC_COLS // LANES)
    sin = _tile_lanes(sin_ref[...], HC_COLS // LANES)
    for g in range(N_GROUPS):
        d = C_DILATIONS[g]
        qn, kn, v = _l1_qkv(u, g, sf_ref[...], gq_ref[...], gk_ref[...], cos, sin)
        kvlast_ref[0, :, g * 2 * HC_COLS:g * 2 * HC_COLS + HC_COLS] = kn
        kvlast_ref[0, :, g * 2 * HC_COLS + HC_COLS:(g + 1) * 2 * HC_COLS] = v
        for t, val in enumerate((qn, kn, v)):
            o_ref = outs[3 * g + t]
            if d == 1:
                o_ref[0, 0] = val.astype(BF16)
            else:
                for cc in range(HC_COLS // LANES):
                    scr[cc] = val[:, cc * LANES:(cc + 1) * LANES]
                for r in range(d):
                    o_ref[0, r] = jnp.concatenate(
                        [scr[cc, pl.ds(r, tm // d, stride=d), :] for cc in range(HC_COLS // LANES)],
                        axis=1).astype(BF16)


def _l1_prompt_proj(x, shift, scale, g, w_in, q_norm, k_norm, batch, seq):
    n, dm = x.shape
    tm = 256
    tps = seq // tm
    wmax = max(C_WINDOWS)
    assert seq >= wmax and wmax % tm == 0
    w = w_in.astype(BF16)
    li = np.arange(LANES)
    sf = jnp.asarray((li[:, None] // HEAD_DIM) == (li[None, :] // HEAD_DIM), BF16)
    gq = (jnp.tile(q_norm, H_C) * HD_SCALE)[None]
    gk = jnp.tile(k_norm, H_C)[None]
    cos, sin = _l1_tables(np.arange(seq))
    tab = pl.BlockSpec((tm, LANES), lambda i: (i % tps, 0))
    first_kept = tps - wmax // tm
    out_specs, out_shapes = [], []
    for gi in range(N_GROUPS):
        d = C_DILATIONS[gi]
        for _ in range(3):
            out_specs.append(pl.BlockSpec((1, d, tm // d, HC_COLS), lambda i: (i // tps, 0, i % tps, 0)))
            out_shapes.append(jax.ShapeDtypeStruct((batch, d, seq // d, HC_COLS), BF16))
    out_specs.append(pl.BlockSpec((1, tm, 2 * N_GROUPS * HC_COLS),
                                  lambda i: (i // tps, jnp.maximum(i % tps - first_kept, 0), 0)))
    out_shapes.append(jax.ShapeDtypeStruct((batch, wmax, 2 * N_GROUPS * HC_COLS), F32))
    ins = [x, shift, scale, g, w, sf, gq, gk, cos, sin]
    return pl.pallas_call(
        functools.partial(_l1_prompt_kernel, tm),
        grid=(n // tm,),
        in_specs=[pl.BlockSpec((tm, dm), lambda i: (i, 0)), _mod_spec(shift, tps), _mod_spec(scale, tps),
                  _const_spec(g), _const_spec(w), _const_spec(sf), _const_spec(gq), _const_spec(gk), tab, tab],
        out_specs=out_specs,
        out_shape=out_shapes,
        scratch_shapes=[pltpu.VMEM((HC_COLS // LANES, tm, LANES), F32)],
        compiler_params=_cparams(("arbitrary",)),
        name="l1_prompt_proj",
    )(*ins)


def _dsa_kernel(bpc, q_ref, kc_ref, kp_ref, vc_ref, vp_ref, num_ref, m_ref, den_ref):
    n = pl.program_id(1)
    tq = q_ref.shape[1]
    lo = jnp.where(n % bpc == 0, tq, 0)
    q = q_ref[0]
    kcat = jnp.concatenate([kp_ref[0], kc_ref[0]], axis=0)
    vcat = jnp.concatenate([vp_ref[0], vc_ref[0]], axis=0)
    a = lax.broadcasted_iota(I32, (tq, 2 * tq), 0)
    c = lax.broadcasted_iota(I32, (tq, 2 * tq), 1)
    ok = (c >= a) & (c <= a + tq) & (c >= lo)
    lane_kv = _lane_iota((2 * tq, LANES))
    lane_o = _lane_iota((tq, LANES))
    m_all = jnp.zeros((tq, LANES), F32)
    den_all = jnp.ones((tq, LANES), F32)
    for p in range(H_C // 2):
        qp = q[:, p * LANES:(p + 1) * LANES]
        kp_ = kcat[:, p * LANES:(p + 1) * LANES]
        vp_ = vcat[:, p * LANES:(p + 1) * LANES]
        num_pair = jnp.zeros((tq, LANES), F32)
        for hh in range(2):
            hm = (lane_kv // HEAD_DIM) == hh
            s = _dot_nt(qp, jnp.where(hm, kp_, jnp.zeros_like(kp_)))
            s = jnp.where(ok, s, NEG)
            m = jnp.max(s, axis=-1, keepdims=True)
            e = jnp.exp(s - m)
            den = jnp.sum(e, axis=-1, keepdims=True)
            num_pair = num_pair + _dot(e, jnp.where(hm, vp_, jnp.zeros_like(vp_)))
            m_all = jnp.where(lane_o == 2 * p + hh, m, m_all)
            den_all = jnp.where(lane_o == 2 * p + hh, den, den_all)
        num_ref[0, :, p * LANES:(p + 1) * LANES] = num_pair
    m_ref[0] = m_all
    den_ref[0] = den_all


def _dsa_attention(q, k, v, dil):
    b, s, _ = q.shape
    tq = PAGE
    bpc = (s // dil) // tq
    cur = pl.BlockSpec((1, tq, HC_COLS), lambda bb, n: (bb, n, 0))
    prev = pl.BlockSpec((1, tq, HC_COLS), lambda bb, n: (bb, jnp.maximum(n - 1, 0), 0))
    stat = pl.BlockSpec((1, tq, LANES), lambda bb, n: (bb, n, 0))
    return pl.pallas_call(
        functools.partial(_dsa_kernel, bpc),
        grid=(b, s // tq),
        in_specs=[cur, cur, prev, cur, prev],
        out_specs=[cur, stat, stat],
        out_shape=[jax.ShapeDtypeStruct((b, s, HC_COLS), F32), jax.ShapeDtypeStruct((b, s, LANES), F32),
                   jax.ShapeDtypeStruct((b, s, LANES), F32)],
        compiler_params=_cparams(("arbitrary", "arbitrary")),
        name="l1_dsa_attention",
    )(q, k, k, v, v)


def _l1_out_kernel(tm, res_ref, gate1_ref, g2_ref, shift2_ref, scale2_ref, wo_ref, wr_ref, br_ref, eh_ref,
                   *rest):
    parts = rest[:3 * N_GROUPS]
    h1_ref, xm_ref, gate_ref, eidx_ref = rest[3 * N_GROUPS:3 * N_GROUPS + 4]
    scr = rest[3 * N_GROUPS + 4:]
    vals = []
    for g in range(N_GROUPS):
        d = C_DILATIONS[g]
        for t in range(3):
            ref, sc = parts[3 * g + t], scr[3 * g + t]
            if d == 1:
                vals.append(ref[0, 0])
            else:
                nch = sc.shape[0]
                for r in range(d):
                    blk = ref[0, r]
                    for cc in range(nch):
                        sc[cc, pl.ds(r, tm // d, stride=d), :] = blk[:, cc * LANES:(cc + 1) * LANES]
                vals.append(sc[0] if nch == 1 else jnp.concatenate([sc[cc] for cc in range(nch)], axis=1))
    nums, ms, dens = vals[0::3], vals[1::3], vals[2::3]
    mx = jnp.maximum(jnp.maximum(ms[0], ms[1]), ms[2])
    ws = [jnp.exp(mm - mx) for mm in ms]
    dsum = ws[0] * dens[0] + ws[1] * dens[1] + ws[2] * dens[2]
    o = jnp.zeros(nums[0].shape, F32)
    for g in range(N_GROUPS):
        o = o + _dot3(ws[g] / dsum, eh_ref[...]) * nums[g]
    h1 = res_ref[...] + gate1_ref[0] * _dot(o, wo_ref[...])
    h1_ref[...] = h1
    _router_epilogue(h1, g2_ref[...], shift2_ref[0], scale2_ref[0], wr_ref, br_ref, xm_ref, gate_ref, eidx_ref)


def _head_expand_mat():
    eh = np.zeros((LANES, HC_COLS), np.float32)
    for h in range(H_C):
        eh[h, h * HEAD_DIM:(h + 1) * HEAD_DIM] = 1.0
    return jnp.asarray(eh, BF16)


def _l1_out(parts, w_out, res, gate1, g2, shift2, scale2, wr, br, batch, seq):
    n = res.shape[0]
    tm = 256
    tps = seq // tm
    wo = w_out.astype(BF16)
    eh = _head_expand_mat()
    ins = [res, gate1, g2, shift2, scale2, wo, wr, br, eh]
    in_specs = [pl.BlockSpec((tm, D_MODEL), lambda i: (i, 0)), _mod_spec(gate1, tps), _const_spec(g2),
                _mod_spec(shift2, tps), _mod_spec(scale2, tps), _const_spec(wo), _const_spec(wr),
                _const_spec(br), _const_spec(eh)]
    scratch = []
    for gi in range(N_GROUPS):
        d = C_DILATIONS[gi]
        for t, arr in enumerate(parts[gi]):
            w = arr.shape[-1]
            ins.append(arr.reshape(batch, d, seq // d, w))
            in_specs.append(pl.BlockSpec((1, d, tm // d, w), lambda i: (i // tps, 0, i % tps, 0)))
            scratch.append(pltpu.VMEM((w // LANES, tm, LANES), F32))
    out_specs, out_shapes = _router_out_specs(n, tm)
    return pl.pallas_call(
        functools.partial(_l1_out_kernel, tm),
        grid=(n // tm,),
        in_specs=in_specs,
        out_specs=out_specs,
        out_shape=out_shapes,
        scratch_shapes=scratch,
        compiler_params=_cparams(("arbitrary",)),
        name="l1_out_router",
    )(*ins)


def _l1_sample_kernel(x_ref, shift_ref, scale_ref, g_ref, w_ref, sf_ref, gq_ref, gk_ref, cos_ref, sin_ref,
                      q_ref, k_ref, v_ref):
    xm = _modulate(x_ref[...], g_ref[...], scale_ref[0], shift_ref[0])
    u = _dot(xm, w_ref[...])
    cos = _tile_lanes(cos_ref[...], HC_COLS // LANES)
    sin = _tile_lanes(sin_ref[...], HC_COLS // LANES)
    for g in range(N_GROUPS):
        qn, kn, v = _l1_qkv(u, g, sf_ref[...], gq_ref[...], gk_ref[...], cos, sin)
        q_ref[:, g * HC_COLS:(g + 1) * HC_COLS] = qn
        k_ref[:, g * HC_COLS:(g + 1) * HC_COLS] = kn
        v_ref[:, g * HC_COLS:(g + 1) * HC_COLS] = v


def _l1_sample_proj(x, shift, scale, g, w_in, q_norm, k_norm, past):
    n = x.shape[0]
    w = w_in.astype(BF16)
    li = np.arange(LANES)
    sf = jnp.asarray((li[:, None] // HEAD_DIM) == (li[None, :] // HEAD_DIM), BF16)
    gq = (jnp.tile(q_norm, H_C) * HD_SCALE)[None]
    gk = jnp.tile(k_norm, H_C)[None]
    cos, sin = _l1_tables(np.full((1,), past))
    ins = [x, shift, scale, g, w, sf, gq, gk, cos, sin]
    wd = N_GROUPS * HC_COLS
    return pl.pallas_call(
        _l1_sample_kernel,
        grid=(1,),
        in_specs=[_const_spec(a) for a in ins],
        out_specs=[pl.BlockSpec((n, wd), lambda i: (0, 0))] * 3,
        out_shape=[jax.ShapeDtypeStruct((n, wd), F32)] * 3,
        compiler_params=_cparams(("arbitrary",)),
        name="l1_sample_proj",
    )(*ins)


DECODE_HEADS_PER_STEP = 4


def _dsa_decode_kernel(q_ref, kcol_ref, vcol_ref, krow_ref, vrow_ref, *rest):
    st = rest[:N_GROUPS]
    o_ref = rest[N_GROUPS]
    new = rest[N_GROUPS + 1:]
    for hh in range(DECODE_HEADS_PER_STEP):
        nums, ms, dens = [], [], []
        for g in range(N_GROUPS):
            w, d = C_WINDOWS[g], C_DILATIONS[g]
            kt = st[g][0, 0, hh]
            vt = st[g][0, 1, hh]
            q = q_ref[0, g, hh]
            s = _dot(jnp.broadcast_to(q, (SUBLANES, HEAD_DIM)), kt)[:1]
            lane = _lane_iota(s.shape)
            s = jnp.where(lane % d == 0, s, NEG)
            s_new = jnp.sum(q * krow_ref[0, g, hh], axis=-1, keepdims=True)
            m = jnp.maximum(jnp.max(s, axis=-1, keepdims=True), s_new)
            e = jnp.exp(s - m)
            e_new = jnp.exp(s_new - m)
            dens.append(jnp.sum(e, axis=-1, keepdims=True) + e_new)
            nums.append(_dot_nt(jnp.broadcast_to(e, (SUBLANES, w)), vt)[:1] + e_new * vrow_ref[0, g, hh])
            ms.append(m)
            lane2 = _lane_iota(kt.shape)
            new[g][0, 0, hh] = jnp.where(lane2 == w - 1, kcol_ref[0, g, hh], pltpu.roll(kt, w - 1, axis=1))
            new[g][0, 1, hh] = jnp.where(lane2 == w - 1, vcol_ref[0, g, hh], pltpu.roll(vt, w - 1, axis=1))
        mx = jnp.maximum(jnp.maximum(ms[0], ms[1]), ms[2])
        ws = [jnp.exp(mm - mx) for mm in ms]
        num = ws[0] * nums[0] + ws[1] * nums[1] + ws[2] * nums[2]
        den = ws[0] * dens[0] + ws[1] * dens[1] + ws[2] * dens[2]
        o_ref[0, hh] = num / den


def _dsa_decode(q, k, v, states):
    bd = q.shape[0]
    hps = DECODE_HEADS_PER_STEP
    q5 = q.reshape(bd, N_GROUPS, H_C, 1, HEAD_DIM)
    krow = k.reshape(bd, N_GROUPS, H_C, 1, HEAD_DIM)
    vrow = v.reshape(bd, N_GROUPS, H_C, 1, HEAD_DIM)
    kcol = k.reshape(bd, N_GROUPS, H_C, HEAD_DIM, 1)
    vcol = v.reshape(bd, N_GROUPS, H_C, HEAD_DIM, 1)
    row_spec = pl.BlockSpec((1, N_GROUPS, hps, 1, HEAD_DIM), lambda b, h: (b, 0, h, 0, 0))
    col_spec = pl.BlockSpec((1, N_GROUPS, hps, HEAD_DIM, 1), lambda b, h: (b, 0, h, 0, 0))
    st_specs = [pl.BlockSpec((1, 2, hps, HEAD_DIM, w), lambda b, h: (b, 0, h, 0, 0)) for w in C_WINDOWS]
    outs = pl.pallas_call(
        _dsa_decode_kernel,
        grid=(bd, H_C // hps),
        in_specs=[row_spec, col_spec, col_spec, row_spec, row_spec] + st_specs,
        out_specs=[pl.BlockSpec((1, hps, 1, HEAD_DIM), lambda b, h: (b, h, 0, 0))] + st_specs,
        out_shape=[jax.ShapeDtypeStruct((bd, H_C, 1, HEAD_DIM), F32)]
        + [jax.ShapeDtypeStruct(s.shape, F32) for s in states],
        compiler_params=_cparams(("arbitrary", "arbitrary")),
        name="l1_dsa_decode",
    )(q5, kcol, vcol, krow, vrow, *states)
    return outs[0].reshape(bd, HC_COLS), outs[1:]


def _out_router_plain(a, w_out, res, gate1, g2, shift2, scale2, wr, br, rows_per_group):
    return _out_router(a, w_out, res, gate1, g2, shift2, scale2, wr, br, rows_per_group)


def kernel(x_prompt, x_sample, cache_mla_ckv, cache_mla_krope, cache_fox_kv, cache_fox_logf, state_c1_kv, state_c2_kv, state_c3_kv, page_table, c_prompt, c_sample, ada_w, ada_b, norm1_g, norm2_g, ab_w_in, mla_qa_norm, mla_kv_norm, mla_w_qb, mla_w_kvb, mla_q_norm, mla_k_norm, fox_q_norm, fox_k_norm, fox_f_bias, ab_w_out, c_w_in, c_q_norm, c_k_norm, c_w_out, router_w, router_b, moe_w1, moe_b1, moe_w2, moe_b2):
    b, s, d = x_prompt.shape
    bd = x_sample.shape[0]
    assert x_sample.shape[1] == 1 and ada_w.shape[0] == 2
    n_pages = page_table.shape[1]
    past = n_pages * PAGE
    assert past >= max(C_WINDOWS) and n_pages % PAGES_PER_STEP == 0
    n = b * s

    pad = (-(b + bd)) % SUBLANES
    c_all = jnp.concatenate([c_prompt, c_sample, jnp.zeros((pad, d), F32)], axis=0)
    mod = _adaln(c_all, ada_w, ada_b)

    def mods(layer):
        mp = [mod[layer, :b, i * d:(i + 1) * d][:, None, :] for i in range(6)]
        ms = [mod[layer, b:b + bd, i * d:(i + 1) * d][None] for i in range(6)]
        return mp, ms

    hp = x_prompt.reshape(n, d)
    hs = x_sample.reshape(bd, d)

    mp, ms = mods(0)
    g1, g2 = norm1_g[0][None], norm2_g[0][None]
    wr, br = _router_consts(router_w[0], router_b[0])
    consts = _l0_consts(ab_w_in[0], mla_qa_norm[0], mla_kv_norm[0], mla_w_qb[0], mla_w_kvb[0], mla_q_norm[0],
                        mla_k_norm[0], fox_q_norm[0], fox_k_norm[0], fox_f_bias[0])
    q, k, v, p_ckv, p_kr, p_fkv, p_lf = _l0_prompt_proj(hp, mp[0], mp[1], g1, consts, s)
    o = _flash_attention(q.reshape(b, s, -1), k.reshape(b, s, -1), v.reshape(b, s, -1))
    h1, xm, gates, eidx = _out_router(o.reshape(n, -1), ab_w_out[0], hp, mp[2], g2, mp[3], mp[4], wr, br, s)
    hp = _moe(h1, xm, gates, eidx, mp[5], s, 0, moe_w1, moe_b1, moe_w2, moe_b2)

    q_s, s_ckv, s_kr, fq_s, s_fkv, s_lf = _l0_sample_proj(hs, ms[0], ms[1], g1, consts, past)
    npool = cache_mla_ckv.shape[1]
    krt_cache = jnp.transpose(cache_mla_krope[0], (0, 2, 1))
    kvt_cache = jnp.transpose(cache_fox_kv[0], (0, 2, 3, 4, 1)).reshape(npool, 2 * KV_B * HEAD_DIM, PAGE)
    lft_cache = jnp.transpose(cache_fox_logf[0], (0, 2, 1))
    oa = _mla_decode(page_table, q_s, s_ckv, s_kr, cache_mla_ckv[0], krt_cache, mla_w_kvb[0], mla_k_norm[0])
    ob = _fox_decode(page_table, fq_s, s_fkv, s_lf, kvt_cache, lft_cache)
    o_s = jnp.concatenate([oa, ob], axis=1).astype(BF16)
    h1, xm, gates, eidx = _out_router(o_s, ab_w_out[0], hs, ms[2], g2, ms[3], ms[4], wr, br, bd)
    hs = _moe(h1, xm, gates, eidx, ms[5], bd, 0, moe_w1, moe_b1, moe_w2, moe_b2)

    mp, ms = mods(1)
    g1, g2 = norm1_g[1][None], norm2_g[1][None]
    wr, br = _router_consts(router_w[1], router_b[1])
    outs = _l1_prompt_proj(hp, mp[0], mp[1], g1, c_w_in[0], c_q_norm[0], c_k_norm[0], b, s)
    kvlast = outs[3 * N_GROUPS]
    parts = []
    for gi in range(N_GROUPS):
        qg, kg, vg = (a.reshape(b, s, HC_COLS) for a in outs[3 * gi:3 * gi + 3])
        parts.append(_dsa_attention(qg, kg, vg, C_DILATIONS[gi]))
    h1, xm, gates, eidx = _l1_out(parts, c_w_out[0], hp, mp[2], g2, mp[3], mp[4], wr, br, b, s)
    hp = _moe(h1, xm, gates, eidx, mp[5], s, 1, moe_w1, moe_b1, moe_w2, moe_b2)

    q1, k1, v1 = _l1_sample_proj(hs, ms[0], ms[1], g1, c_w_in[0], c_q_norm[0], c_k_norm[0], past)
    states = [jnp.transpose(st[0], (0, 2, 3, 4, 1)) for st in (state_c1_kv, state_c2_kv, state_c3_kv)]
    o1, new_states = _dsa_decode(q1, k1, v1, states)
    h1, xm, gates, eidx = _out_router(o1.astype(BF16), c_w_out[0], hs, ms[2], g2, ms[3], ms[4], wr, br, bd)
    hs = _moe(h1, xm, gates, eidx, ms[5], bd, 1, moe_w1, moe_b1, moe_w2, moe_b2)

    wmax = max(C_WINDOWS)
    p_c = []
    for gi, w in enumerate(C_WINDOWS):
        blk = kvlast[:, wmax - w:, gi * 2 * HC_COLS:(gi + 1) * 2 * HC_COLS]
        p_c.append(blk.reshape(1, b, w, 2, H_C, HEAD_DIM))
    s_c = [jnp.transpose(ns, (0, 4, 1, 2, 3))[None] for ns in new_states]
    return (hp.reshape(b, s, d), hs.reshape(bd, 1, d),
            p_ckv.reshape(1, b, s, KV_LORA), p_kr.reshape(1, b, s, ROPE_D),
            p_fkv.reshape(1, b, s, 2, KV_B, HEAD_DIM), p_lf.reshape(1, b, s, H_B),
            p_c[0], p_c[1], p_c[2],
            s_ckv.reshape(1, bd, 1, KV_LORA), s_kr[:, :ROPE_D].reshape(1, bd, 1, ROPE_D),
            s_fkv.reshape(1, bd, 1, 2, KV_B, HEAD_DIM), s_lf[:, :H_B].reshape(1, bd, 1, H_B),
            s_c[0], s_c[1], s_c[2])
```
